```python
import jax, jax.numpy as jnp
from jax import lax
import numpy as np

D_MODEL = 4096
BATCH = 1
SEQ = 8192
DEPTH = 2

D_CONF = D_MODEL // 2
D_SCONV = D_MODEL // 2
N_GROUPS = 16
CONF_KERNEL = 31
SCONV_KERNEL = 3
FFN_KERNEL = 3
D_FF = ((8 * D_MODEL // 3 + 255) // 256) * 256
SPLITS = (D_CONF, D_CONF, D_SCONV, D_SCONV, D_SCONV, D_MODEL, D_MODEL)
N_IN = sum(SPLITS)
N_ADA = 6 * D_MODEL
RMS_EPS = 1e-6
LN_EPS = 1e-5

kernel_name = "hybrid_conformer_shortconv_convffn_adaln"


def rms_norm(x, g):
    xf = x.astype(jnp.float32)
    y = xf * lax.rsqrt(jnp.mean(xf * xf, axis=-1, keepdims=True) + RMS_EPS)
    return (y * g.astype(jnp.float32)).astype(x.dtype)


def layer_norm(x, g, b):
    xf = x.astype(jnp.float32)
    mu = jnp.mean(xf, axis=-1, keepdims=True)
    var = jnp.mean(jnp.square(xf - mu), axis=-1, keepdims=True)
    y = (xf - mu) * lax.rsqrt(var + LN_EPS)
    return (y * g.astype(jnp.float32) + b.astype(jnp.float32)).astype(x.dtype)


def causal_dwconv(x, w):
    k = w.shape[0]
    c = x.shape[-1]
    return lax.conv_general_dilated(
        x, w[:, None, :].astype(x.dtype), window_strides=(1,), padding=[(k - 1, 0)],
        dimension_numbers=("NWC", "WIO", "NWC"), feature_group_count=c)


def modulate(h, shift, scale):
    return h * (1.0 + scale[:, None, :]) + shift[:, None, :]


def setup_inputs(seed: int = 0) -> dict:
    key = jax.random.key(seed)
    ks = jax.random.split(key, 24)
    f32 = jnp.float32
    L, D = DEPTH, D_MODEL

    def nrm(k, shape, scale):
        return jax.random.normal(k, shape, f32) * scale

    return {
        "x": nrm(ks[0], (BATCH, SEQ, D), 1.0),
        "c": nrm(ks[1], (BATCH, D), 1.0),
        "w_ada": nrm(ks[2], (L, D, N_ADA), 0.5 * D ** -0.5),
        "b_ada": nrm(ks[3], (L, N_ADA), 0.01),
        "g_mix": 1.0 + nrm(ks[4], (L, D), 0.02),
        "w_in": nrm(ks[5], (L, D, N_IN), D ** -0.5),
        "conf_w": nrm(ks[6], (L, CONF_KERNEL, D_CONF), CONF_KERNEL ** -0.5),
        "conf_b": nrm(ks[7], (L, D_CONF), 0.01),
        "ln_g": 1.0 + nrm(ks[8], (L, D_CONF), 0.02),
        "ln_b": nrm(ks[9], (L, D_CONF), 0.01),
        "w_a_out": nrm(ks[10], (L, D_CONF, D), D_CONF ** -0.5),
        "sconv_w": nrm(ks[11], (L, SCONV_KERNEL, D_SCONV), SCONV_KERNEL ** -0.5),
        "w_b_out": nrm(ks[12], (L, D_SCONV, D), D_SCONV ** -0.5),
        "w_o": nrm(ks[13], (L, D, D), D ** -0.5),
        "g_ffn": 1.0 + nrm(ks[14], (L, D), 0.02),
        "w_up": nrm(ks[15], (L, D, 2 * D_FF), D ** -0.5),
        "ffn_conv_w": nrm(ks[16], (L, FFN_KERNEL, 2 * D_FF), FFN_KERNEL ** -0.5),
        "w_down": nrm(ks[17], (L, D_FF, D), D_FF ** -0.5),
        "g_final": 1.0 + nrm(ks[18], (D,), 0.02),
    }


def reference(x, c, w_ada, b_ada, g_mix, w_in, conf_w, conf_b, ln_g, ln_b, w_a_out,
              sconv_w, w_b_out, w_o, g_ffn, w_up, ffn_conv_w, w_down, g_final):
    split_idx = [int(v) for v in np.cumsum(SPLITS)[:-1]]
    c_act = jax.nn.silu(c)
    for l in range(DEPTH):
        mod = c_act @ w_ada[l] + b_ada[l]
        sh1, sc1, gt1, sh2, sc2, gt2 = jnp.split(mod, 6, axis=-1)

        h = modulate(rms_norm(x, g_mix[l]), sh1, sc1)
        z = h @ w_in[l]
        a_v, a_g, s_b, s_c, s_x, pre_ga, pre_gb = jnp.split(z, split_idx, axis=-1)

        u = a_v * jax.nn.sigmoid(a_g)
        u = causal_dwconv(u, conf_w[l]) + conf_b[l]
        u = jax.nn.silu(layer_norm(u, ln_g[l], ln_b[l]))
        y_a = u @ w_a_out[l]

        v = causal_dwconv(s_c * s_x, sconv_w[l])
        y_b = (s_b * v) @ w_b_out[l]

        m = jax.nn.sigmoid(pre_ga) * y_a + jax.nn.sigmoid(pre_gb) * y_b
        x = x + gt1[:, None, :] * (m @ w_o[l])

        h = modulate(rms_norm(x, g_ffn[l]), sh2, sc2)
        up = causal_dwconv(h @ w_up[l], ffn_conv_w[l])
        f_gate, f_val = jnp.split(up, 2, axis=-1)
        x = x + gt2[:, None, :] * ((jax.nn.silu(f_gate) * f_val) @ w_down[l])

    return rms_norm(x, g_final)
```

```python
import functools

import jax
import jax.numpy as jnp
from jax import lax
from jax.experimental import pallas as pl
from jax.experimental.pallas import tpu as pltpu

F32 = jnp.float32
BF16 = jnp.bfloat16

RMS_EPS = 1e-6
LN_EPS = 1e-5
CONF_KERNEL = 31
SCONV_KERNEL = 3
FFN_KERNEL = 3

LANES = 128
SUBLANES = 8
VMEM_LIMIT = 56 * 1024 * 1024

ROW_CHUNK = 256
CONF_PAD = 32
SHORT_PAD = 8


def _params(sem):
    return pltpu.CompilerParams(dimension_semantics=sem, vmem_limit_bytes=VMEM_LIMIT)


def _sigmoid(x):
    return jax.nn.sigmoid(x)


def _dot(a, b):
    return jnp.dot(a, b, preferred_element_type=F32)


def _ada_kernel(c_ref, w_ref, b_ref, o_ref):
    c = c_ref[...]
    ca = (c * _sigmoid(c)).astype(BF16)
    o_ref[...] = _dot(ca, w_ref[...].astype(BF16)) + b_ref[...]


def _ada_mod(c, w_ada, b_ada):
    n_layers, d, n = w_ada.shape
    tn = 1024
    c8 = jnp.broadcast_to(c.reshape(1, d), (SUBLANES, d))
    out = pl.pallas_call(
        _ada_kernel,
        grid=(n_layers, n // tn),
        in_specs=[
            pl.BlockSpec((SUBLANES, d), lambda l, j: (0, 0)),
            pl.BlockSpec((None, d, tn), lambda l, j: (l, 0, j)),
            pl.BlockSpec((None, 1, tn), lambda l, j: (l, 0, j)),
        ],
        out_specs=pl.BlockSpec((None, SUBLANES, tn), lambda l, j: (l, 0, j)),
        out_shape=jax.ShapeDtypeStruct((n_layers, SUBLANES, n), F32),
        compiler_params=_params(("arbitrary", "arbitrary")),
        name="ada_mod",
    )(c8, w_ada, b_ada.reshape(n_layers, 1, n))
    return out[:, 0, :]


def _norm_mod_kernel(x_ref, g_ref, sc_ref, sh_ref, o_ref):
    x = x_ref[...]
    ms = jnp.mean(x * x, axis=-1, keepdims=True)
    y = x * lax.rsqrt(ms + RMS_EPS) * g_ref[...]
    o_ref[...] = (y * (1.0 + sc_ref[...]) + sh_ref[...]).astype(o_ref.dtype)


def _norm_kernel(x_ref, g_ref, o_ref):
    x = x_ref[...]
    ms = jnp.mean(x * x, axis=-1, keepdims=True)
    o_ref[...] = (x * lax.rsqrt(ms + RMS_EPS) * g_ref[...]).astype(o_ref.dtype)


def _norm_mod(x, g, sc, sh):
    s, d = x.shape
    tr = 256
    row = pl.BlockSpec((tr, d), lambda i: (i, 0))
    vec = pl.BlockSpec((1, d), lambda i: (0, 0))
    return pl.pallas_call(
        _norm_mod_kernel, grid=(s // tr,),
        in_specs=[row, vec, vec, vec], out_specs=row,
        out_shape=jax.ShapeDtypeStruct((s, d), BF16),
        compiler_params=_params(("arbitrary",)), name="norm_mod",
    )(x, g, sc, sh)


def _final_norm(x, g):
    s, d = x.shape
    tr = 256
    row = pl.BlockSpec((tr, d), lambda i: (i, 0))
    vec = pl.BlockSpec((1, d), lambda i: (0, 0))
    return pl.pallas_call(
        _norm_kernel, grid=(s // tr,),
        in_specs=[row, vec], out_specs=row,
        out_shape=jax.ShapeDtypeStruct((s, d), F32),
        compiler_params=_params(("arbitrary",)), name="final_norm",
    )(x, g)


def _load_halo(e_ref, halo_ref, pad):
    i = pl.program_id(0)
    j = pl.program_id(1)

    @pl.when(i == 0)
    def _():
        e_ref[:, 0:pad, :] = jnp.zeros((e_ref.shape[0], pad, LANES), F32)

    @pl.when(i > 0)
    def _():
        e_ref[:, 0:pad, :] = halo_ref[j]


def _save_halo(e_ref, halo_ref, pad, tm):
    halo_ref[pl.program_id(1)] = e_ref[:, tm:tm + pad, :]


def _put_rows(e_ref, u, slab0, row0):
    for c in range(u.shape[1] // LANES):
        e_ref[slab0 + c, row0:row0 + u.shape[0], :] = u[:, c * LANES:(c + 1) * LANES]


def _conv_slab(e_ref, cw_ref, slab, col, row0, rows, k_taps):
    acc = None
    for kk in range(k_taps):
        shift = k_taps - 1 - kk
        term = e_ref[slab, row0 - shift:row0 - shift + rows, :] * cw_ref[kk:kk + 1, col * LANES:(col + 1) * LANES]
        acc = term if acc is None else acc + term
    return acc


def _proj_conv_kernel(h_ref, w1_ref, w2_ref, cw_ref, cb_ref, o_ref, e_ref, halo_ref, *, glu, k_taps, pad):
    tm, tn = o_ref.shape
    ncb = tn // LANES
    _load_halo(e_ref, halo_ref, pad)
    for r0 in range(0, tm, ROW_CHUNK):
        hh = h_ref[r0:r0 + ROW_CHUNK, :]
        t1 = _dot(hh, w1_ref[...])
        t2 = _dot(hh, w2_ref[...])
        u = t1 * _sigmoid(t2) if glu else t1 * t2
        _put_rows(e_ref, u, 0, pad + r0)
        for c in range(ncb):
            y = _conv_slab(e_ref, cw_ref, c, c, pad + r0, ROW_CHUNK, k_taps)
            if cb_ref is not None:
                y = y + cb_ref[:, c * LANES:(c + 1) * LANES]
            o_ref[r0:r0 + ROW_CHUNK, c * LANES:(c + 1) * LANES] = y.astype(o_ref.dtype)
    _save_halo(e_ref, halo_ref, pad, tm)


def _proj_conv(h, w, layer, blk1, blk2, conv_w, conv_b, *, glu, k_taps, pad, n_out, tm=1024, tn=512):
    s, d = h.shape
    nj = n_out // tn
    ncb = tn // LANES
    in_specs = [
        pl.BlockSpec((tm, d), lambda i, j: (i, 0)),
        pl.BlockSpec((None, d, tn), lambda i, j: (layer, 0, blk1 + j)),
        pl.BlockSpec((None, d, tn), lambda i, j: (layer, 0, blk2 + j)),
        pl.BlockSpec((None, k_taps, tn), lambda i, j: (layer, 0, j)),
    ]
    args = [h, w, w, conv_w]
    if conv_b is not None:
        in_specs.append(pl.BlockSpec((None, 1, tn), lambda i, j: (layer, 0, j)))
        args.append(conv_b)
        body = functools.partial(_proj_conv_kernel, glu=glu, k_taps=k_taps, pad=pad)
    else:
        def body(h_ref, w1_ref, w2_ref, cw_ref, o_ref, e_ref, halo_ref):
            _proj_conv_kernel(h_ref, w1_ref, w2_ref, cw_ref, None, o_ref, e_ref, halo_ref,
                              glu=glu, k_taps=k_taps, pad=pad)
    return pl.pallas_call(
        body, grid=(s // tm, nj),
        in_specs=in_specs,
        out_specs=pl.BlockSpec((tm, tn), lambda i, j: (i, j)),
        out_shape=jax.ShapeDtypeStruct((s, n_out), BF16),
        scratch_shapes=[pltpu.VMEM((ncb, pad + tm, LANES), F32),
                        pltpu.VMEM((nj, ncb, pad, LANES), F32)],
        compiler_params=_params(("arbitrary", "arbitrary")),
        name="proj_glu_conv31" if glu else "proj_mul_conv3",
    )(*args)


def _proj_mul_kernel(h_ref, w_ref, v_ref, o_ref):
    tm = o_ref.shape[0]
    for r0 in range(0, tm, ROW_CHUNK):
        t = _dot(h_ref[r0:r0 + ROW_CHUNK, :], w_ref[...])
        o_ref[r0:r0 + ROW_CHUNK, :] = (t * v_ref[r0:r0 + ROW_CHUNK, :].astype(F32)).astype(o_ref.dtype)


def _proj_sigmoid_kernel(h_ref, w_ref, o_ref):
    tm = o_ref.shape[0]
    for r0 in range(0, tm, ROW_CHUNK):
        t = _dot(h_ref[r0:r0 + ROW_CHUNK, :], w_ref[...])
        o_ref[r0:r0 + ROW_CHUNK, :] = _sigmoid(t).astype(o_ref.dtype)


def _proj_single(h, w, layer, blk, n_out, v=None, tm=1024, tn=1024):
    s, d = h.shape
    in_specs = [
        pl.BlockSpec((tm, d), lambda i, j: (i, 0)),
        pl.BlockSpec((None, d, tn), lambda i, j: (layer, 0, blk + j)),
    ]
    args = [h, w]
    if v is not None:
        in_specs.append(pl.BlockSpec((tm, tn), lambda i, j: (i, j)))
        args.append(v)
    return pl.pallas_call(
        _proj_mul_kernel if v is not None else _proj_sigmoid_kernel,
        grid=(s // tm, n_out // tn),
        in_specs=in_specs,
        out_specs=pl.BlockSpec((tm, tn), lambda i, j: (i, j)),
        out_shape=jax.ShapeDtypeStruct((s, n_out), BF16),
        compiler_params=_params(("arbitrary", "arbitrary")),
        name="proj_mul" if v is not None else "proj_sigmoid",
    )(*args)


def _mixer_kernel(c_ref, vb_ref, lng_ref, lnb_ref, ga_ref, gb_ref, wa_ref, wb_ref, o_ref, ua_ref):
    tm = o_ref.shape[0]

    @pl.when(pl.program_id(1) == 0)
    def _():
        for r0 in range(0, tm, ROW_CHUNK):
            cc = c_ref[r0:r0 + ROW_CHUNK, :].astype(F32)
            mu = jnp.mean(cc, axis=-1, keepdims=True)
            dev = cc - mu
            var = jnp.mean(dev * dev, axis=-1, keepdims=True)
            y = dev * lax.rsqrt(var + LN_EPS) * lng_ref[...] + lnb_ref[...]
            ua_ref[r0:r0 + ROW_CHUNK, :] = (y * _sigmoid(y)).astype(ua_ref.dtype)

    for r0 in range(0, tm, ROW_CHUNK):
        ya = _dot(ua_ref[r0:r0 + ROW_CHUNK, :], wa_ref[...])
        yb = _dot(vb_ref[r0:r0 + ROW_CHUNK, :], wb_ref[...])
        m = ga_ref[r0:r0 + ROW_CHUNK, :].astype(F32) * ya + gb_ref[r0:r0 + ROW_CHUNK, :].astype(F32) * yb
        o_ref[r0:r0 + ROW_CHUNK, :] = m.astype(o_ref.dtype)


def _mixer(cconv, vb, ln_g, ln_b, gates, w_a, w_b, layer, tm=1024, tn=1024):
    s, dc = cconv.shape
    d = w_a.shape[2]
    nj = d // tn
    return pl.pallas_call(
        _mixer_kernel, grid=(s // tm, nj),
        in_specs=[
            pl.BlockSpec((tm, dc), lambda i, j: (i, 0)),
            pl.BlockSpec((tm, dc), lambda i, j: (i, 0)),
            pl.BlockSpec((None, 1, dc), lambda i, j: (layer, 0, 0)),
            pl.BlockSpec((None, 1, dc), lambda i, j: (layer, 0, 0)),
            pl.BlockSpec((tm, tn), lambda i, j: (i, j)),
            pl.BlockSpec((tm, tn), lambda i, j: (i, nj + j)),
            pl.BlockSpec((None, dc, tn), lambda i, j: (layer, 0, j)),
            pl.BlockSpec((None, dc, tn), lambda i, j: (layer, 0, j)),
        ],
        out_specs=pl.BlockSpec((tm, tn), lambda i, j: (i, j)),
        out_shape=jax.ShapeDtypeStruct((s, d), BF16),
        scratch_shapes=[pltpu.VMEM((tm, dc), BF16)],
        compiler_params=_params(("arbitrary", "arbitrary")),
        name="mixer",
    )(cconv, vb, ln_g, ln_b, gates, gates, w_a, w_b)


def _resid_kernel(a_ref, w_ref, x_ref, gt_ref, o_ref, *, rc):
    tm = o_ref.shape[0]
    for r0 in range(0, tm, rc):
        t = _dot(a_ref[r0:r0 + rc, :], w_ref[...])
        o_ref[r0:r0 + rc, :] = x_ref[r0:r0 + rc, :] + gt_ref[...] * t


def _resid_proj(a, w, layer, x, gt, tm, tn, rc=ROW_CHUNK):
    s, k = a.shape
    d = x.shape[1]
    return pl.pallas_call(
        functools.partial(_resid_kernel, rc=rc), grid=(s // tm, d // tn),
        in_specs=[
            pl.BlockSpec((tm, k), lambda i, j: (i, 0)),
            pl.BlockSpec((None, k, tn), lambda i, j: (layer, 0, j)),
            pl.BlockSpec((tm, tn), lambda i, j: (i, j)),
            pl.BlockSpec((1, tn), lambda i, j: (0, j)),
        ],
        out_specs=pl.BlockSpec((tm, tn), lambda i, j: (i, j)),
        out_shape=jax.ShapeDtypeStruct((s, d), F32),
        compiler_params=_params(("arbitrary", "arbitrary")),
        name="resid_proj",
    )(a, w, x, gt)


def _ffn_up_kernel(h_ref, wg_ref, wv_ref, cwg_ref, cwv_ref, o_ref, e_ref, halo_ref):
    tm, tn = o_ref.shape
    ncb = tn // LANES
    pad = SHORT_PAD
    _load_halo(e_ref, halo_ref, pad)
    for r0 in range(0, tm, ROW_CHUNK):
        hh = h_ref[r0:r0 + ROW_CHUNK, :]
        _put_rows(e_ref, _dot(hh, wg_ref[...]), 0, pad + r0)
        _put_rows(e_ref, _dot(hh, wv_ref[...]), ncb, pad + r0)
        for c in range(ncb):
            fg = _conv_slab(e_ref, cwg_ref, c, c, pad + r0, ROW_CHUNK, FFN_KERNEL)
            fv = _conv_slab(e_ref, cwv_ref, ncb + c, c, pad + r0, ROW_CHUNK, FFN_KERNEL)
            o_ref[r0:r0 + ROW_CHUNK, c * LANES:(c + 1) * LANES] = (fg * _sigmoid(fg) * fv).astype(o_ref.dtype)
    _save_halo(e_ref, halo_ref, pad, tm)


def _ffn_up(h, w_up, conv_w, layer, d_ff, tm=1024, tn=256):
    s, d = h.shape
    nj = d_ff // tn
    ncb = tn // LANES
    pad = SHORT_PAD
    return pl.pallas_call(
        _ffn_up_kernel, grid=(s // tm, nj),
        in_specs=[
            pl.BlockSpec((tm, d), lambda i, j: (i, 0)),
            pl.BlockSpec((None, d, tn), lambda i, j: (layer, 0, j)),
            pl.BlockSpec((None, d, tn), lambda i, j: (layer, 0, nj + j)),
            pl.BlockSpec((None, FFN_KERNEL, tn), lambda i, j: (layer, 0, j)),
            pl.BlockSpec((None, FFN_KERNEL, tn), lambda i, j: (layer, 0, nj + j)),
        ],
        out_specs=pl.BlockSpec((tm, tn), lambda i, j: (i, j)),
        out_shape=jax.ShapeDtypeStruct((s, d_ff), BF16),
        scratch_shapes=[pltpu.VMEM((2 * ncb, pad + tm, LANES), F32),
                        pltpu.VMEM((nj, 2 * ncb, pad, LANES), F32)],
        compiler_params=_params(("arbitrary", "arbitrary")),
        name="ffn_up",
    )(h, w_up, w_up, conv_w, conv_w)


def kernel(x, c, w_ada, b_ada, g_mix, w_in, conf_w, conf_b, ln_g, ln_b, w_a_out, sconv_w, w_b_out, w_o, g_ffn, w_up, ffn_conv_w, w_down, g_final):
    batch, seq, d = x.shape
    assert batch == 1
    n_layers = w_ada.shape[0]
    d_conf = conf_w.shape[2]
    d_ff = w_down.shape[1]
    tn_pair = 512
    nb = d_conf // tn_pair

    mod = _ada_mod(c, w_ada, b_ada).reshape(n_layers, 6, 1, d)

    w_in_b = w_in.astype(BF16)
    w_a_b = w_a_out.astype(BF16)
    w_b_b = w_b_out.astype(BF16)
    w_o_b = w_o.astype(BF16)
    w_up_b = w_up.astype(BF16)
    w_down_b = w_down.astype(BF16)
    conf_b3 = conf_b.reshape(n_layers, 1, d_conf)
    ln_g3 = ln_g.reshape(n_layers, 1, d_conf)
    ln_b3 = ln_b.reshape(n_layers, 1, d_conf)

    xs = x.reshape(seq, d)
    for l in range(n_layers):
        sh1, sc1, gt1, sh2, sc2, gt2 = (mod[l, k] for k in range(6))

        h = _norm_mod(xs, g_mix[l].reshape(1, d), sc1, sh1)
        cconv = _proj_conv(h, w_in_b, l, 0, nb, conf_w, conf_b3, glu=True, k_taps=CONF_KERNEL,
                           pad=CONF_PAD, n_out=d_conf)
        v = _proj_conv(h, w_in_b, l, 3 * nb, 4 * nb, sconv_w, None, glu=False, k_taps=SCONV_KERNEL,
                       pad=SHORT_PAD, n_out=d_conf)
        vb = _proj_single(h, w_in_b, l, 2 * d_conf // 1024, d_conf, v=v)
        gates = _proj_single(h, w_in_b, l, 5 * d_conf // 1024, 2 * d)
        m = _mixer(cconv, vb, ln_g3, ln_b3, gates, w_a_b, w_b_b, l)
        xs = _resid_proj(m, w_o_b, l, xs, gt1, tm=1024, tn=1024)

        h2 = _norm_mod(xs, g_ffn[l].reshape(1, d), sc2, sh2)
        g = _ffn_up(h2, w_up_b, ffn_conv_w, l, d_ff)
        xs = _resid_proj(g, w_down_b, l, xs, gt2, tm=512, tn=512)

    return _final_norm(xs, g_final.reshape(1, d)).reshape(batch, seq, d)
```

```python
import collections
import functools

import jax
import jax.numpy as jnp
from jax import lax
from jax.experimental import pallas as pl
from jax.experimental.pallas import tpu as pltpu

F32 = jnp.float32
BF16 = jnp.bfloat16

RMS_EPS = 1e-6
LN_EPS = 1e-5
CONF_KERNEL = 31
SCONV_KERNEL = 3
FFN_KERNEL = 3

LANES = 128
SUBLANES = 8
VMEM_LIMIT = 56 * 1024 * 1024

ROW_CHUNK = 256
CONF_PAD = 32
SHORT_PAD = 8
LN_ROWS = 64


def _params(sem):
    return pltpu.CompilerParams(dimension_semantics=sem, vmem_limit_bytes=VMEM_LIMIT)


def _sigmoid(x):
    return jax.nn.sigmoid(x)


def _dot(a, b):
    return jnp.dot(a, b, preferred_element_type=F32)


CastJob = collections.namedtuple("CastJob", "w layer col_block cols")
BF16_ROWS = 16


def _cast_rows(n_rows, n_steps):
    rows = BF16_ROWS
    while n_rows % rows or n_rows // rows > n_steps:
        rows += BF16_ROWS
    return rows


def _host_call(body, *, grid, in_specs, args, out_specs, out_shapes, scratch=(), casts=(), name):
    gi, gj = grid
    n_in, n_out, n_cast = len(args), len(out_specs), len(casts)
    in_specs, out_specs, out_shapes, args = list(in_specs), list(out_specs), list(out_shapes), list(args)
    for job in casts:
        n_rows = job.w.shape[1]
        rows = _cast_rows(n_rows, gi * gj)
        n_blocks = n_rows // rows

        def block(i, j, n_blocks=n_blocks):
            return jnp.minimum(i * gj + j, n_blocks - 1)

        in_specs.append(pl.BlockSpec((None, rows, job.cols),
                                     lambda i, j, job=job, block=block: (job.layer, block(i, j), job.col_block)))
        out_specs.append(pl.BlockSpec((rows, job.cols), lambda i, j, block=block: (block(i, j), 0)))
        out_shapes.append(jax.ShapeDtypeStruct((n_rows, job.cols), BF16))
        args.append(job.w)

    def wrapped(*refs):
        main_in = refs[:n_in]
        cast_in = refs[n_in:n_in + n_cast]
        main_out = refs[n_in + n_cast:n_in + n_cast + n_out]
        cast_out = refs[n_in + n_cast + n_out:n_in + 2 * n_cast + n_out]
        rest = refs[n_in + 2 * n_cast + n_out:]
        for src, dst in zip(cast_in, cast_out):
            dst[...] = src[...].astype(BF16)
        body(*main_in, *main_out, *rest)

    outs = pl.pallas_call(
        wrapped, grid=grid, in_specs=in_specs, out_specs=out_specs, out_shape=out_shapes,
        scratch_shapes=list(scratch), compiler_params=_params(("arbitrary", "arbitrary")), name=name,
    )(*args)
    return outs[:n_out], outs[n_out:]


def _ada_kernel(c_ref, w_ref, b_ref, o_ref):
    c = c_ref[...]
    ca = (c * _sigmoid(c)).astype(BF16)
    o_ref[...] = _dot(ca, w_ref[...].astype(BF16)) + b_ref[...]


def _ada_mod(c, w_ada, b_ada):
    n_layers, d, n = w_ada.shape
    tn = 1024
    c8 = jnp.broadcast_to(c.reshape(1, d), (SUBLANES, d))
    out = pl.pallas_call(
        _ada_kernel,
        grid=(n_layers, n // tn),
        in_specs=[
            pl.BlockSpec((SUBLANES, d), lambda l, j: (0, 0)),
            pl.BlockSpec((None, d, tn), lambda l, j: (l, 0, j)),
            pl.BlockSpec((None, 1, tn), lambda l, j: (l, 0, j)),
        ],
        out_specs=pl.BlockSpec((None, SUBLANES, tn), lambda l, j: (l, 0, j)),
        out_shape=jax.ShapeDtypeStruct((n_layers, SUBLANES, n), F32),
        compiler_params=_params(("arbitrary", "arbitrary")),
        name="ada_mod",
    )(c8, w_ada, b_ada.reshape(n_layers, 1, n))
    return out[:, 0, :]


def _norm_mod_kernel(x_ref, g_ref, sc_ref, sh_ref, o_ref):
    x = x_ref[...]
    ms = jnp.mean(x * x, axis=-1, keepdims=True)
    y = x * lax.rsqrt(ms + RMS_EPS) * g_ref[...]
    o_ref[...] = (y * (1.0 + sc_ref[...]) + sh_ref[...]).astype(o_ref.dtype)


def _norm_kernel(x_ref, g_ref, o_ref):
    x = x_ref[...]
    ms = jnp.mean(x * x, axis=-1, keepdims=True)
    o_ref[...] = (x * lax.rsqrt(ms + RMS_EPS) * g_ref[...]).astype(o_ref.dtype)


def _norm_mod(x, g, sc, sh):
    s, d = x.shape
    tr = 512
    row = pl.BlockSpec((tr, d), lambda i: (i, 0))
    vec = pl.BlockSpec((1, d), lambda i: (0, 0))
    return pl.pallas_call(
        _norm_mod_kernel, grid=(s // tr,),
        in_specs=[row, vec, vec, vec], out_specs=row,
        out_shape=jax.ShapeDtypeStruct((s, d), BF16),
        compiler_params=_params(("arbitrary",)), name="norm_mod",
    )(x, g, sc, sh)


def _final_norm(x, g):
    s, d = x.shape
    tr = 512
    row = pl.BlockSpec((tr, d), lambda i: (i, 0))
    vec = pl.BlockSpec((1, d), lambda i: (0, 0))
    return pl.pallas_call(
        _norm_kernel, grid=(s // tr,),
        in_specs=[row, vec], out_specs=row,
        out_shape=jax.ShapeDtypeStruct((s, d), F32),
        compiler_params=_params(("arbitrary",)), name="final_norm",
    )(x, g)


def _load_halo(e_ref, halo_ref, pad):
    i = pl.program_id(0)
    j = pl.program_id(1)

    @pl.when(i == 0)
    def _():
        e_ref[:, 0:pad, :] = jnp.zeros((e_ref.shape[0], pad, LANES), F32)

    @pl.when(i > 0)
    def _():
        e_ref[:, 0:pad, :] = halo_ref[j]


def _save_halo(e_ref, halo_ref, pad, tm):
    halo_ref[pl.program_id(1)] = e_ref[:, tm:tm + pad, :]


def _put_rows(e_ref, u, slab0, row0):
    for c in range(u.shape[1] // LANES):
        e_ref[slab0 + c, row0:row0 + u.shape[0], :] = u[:, c * LANES:(c + 1) * LANES]


def _conv_slab(e_ref, cw_ref, slab, col, row0, rows, k_taps):
    acc = None
    for kk in range(k_taps):
        shift = k_taps - 1 - kk
        term = e_ref[slab, row0 - shift:row0 - shift + rows, :] * cw_ref[kk:kk + 1, col * LANES:(col + 1) * LANES]
        acc = term if acc is None else acc + term
    return acc


def _proj_glu_kernel(h_ref, w1_ref, w2_ref, o_ref):
    tm = o_ref.shape[0]
    for r0 in range(0, tm, ROW_CHUNK):
        hh = h_ref[r0:r0 + ROW_CHUNK, :]
        t1 = _dot(hh, w1_ref[...])
        t2 = _dot(hh, w2_ref[...])
        o_ref[r0:r0 + ROW_CHUNK, :] = (t1 * _sigmoid(t2)).astype(o_ref.dtype)


def _proj_glu(h, w, blk1, blk2, n_out, casts, tm=1024, tn=512):
    s, d = h.shape
    (u,), cast_out = _host_call(
        _proj_glu_kernel, grid=(s // tm, n_out // tn),
        in_specs=[
            pl.BlockSpec((tm, d), lambda i, j: (i, 0)),
            pl.BlockSpec((d, tn), lambda i, j: (0, blk1 + j)),
            pl.BlockSpec((d, tn), lambda i, j: (0, blk2 + j)),
        ],
        args=[h, w, w],
        out_specs=[pl.BlockSpec((tm, tn), lambda i, j: (i, j))],
        out_shapes=[jax.ShapeDtypeStruct((s, n_out), BF16)],
        casts=casts, name="proj_glu")
    return u, cast_out


def _proj_sigmoid_conv_kernel(h_ref, w_ref, um_ref, uh_ref, cw_ref, cb_ref, o_ref, c_ref, e_ref):
    tm = o_ref.shape[0]
    ncb = c_ref.shape[1] // LANES
    pad = CONF_PAD
    first = pl.program_id(0) == 0
    _put_rows(e_ref, jnp.where(first, 0.0, uh_ref[...].astype(F32)), 0, 0)
    for r0 in range(0, tm, ROW_CHUNK):
        _put_rows(e_ref, um_ref[r0:r0 + ROW_CHUNK, :].astype(F32), 0, pad + r0)
    for r0 in range(0, tm, ROW_CHUNK):
        t = _dot(h_ref[r0:r0 + ROW_CHUNK, :], w_ref[...])
        o_ref[r0:r0 + ROW_CHUNK, :] = _sigmoid(t).astype(o_ref.dtype)
        for c in range(ncb):
            y = _conv_slab(e_ref, cw_ref, c, c, pad + r0, ROW_CHUNK, CONF_KERNEL)
            y = y + cb_ref[:, c * LANES:(c + 1) * LANES]
            c_ref[r0:r0 + ROW_CHUNK, c * LANES:(c + 1) * LANES] = y.astype(c_ref.dtype)


def _proj_sigmoid_conv(h, w, blk, n_out, u, conv_w, conv_b, layer, casts, tm=1024, tn=1024):
    s, d = h.shape
    dc = u.shape[1]
    nj = n_out // tn
    tc = dc // nj
    halo_blocks = tm // CONF_PAD
    (gates, cconv), cast_out = _host_call(
        _proj_sigmoid_conv_kernel, grid=(s // tm, nj),
        in_specs=[
            pl.BlockSpec((tm, d), lambda i, j: (i, 0)),
            pl.BlockSpec((d, tn), lambda i, j: (0, blk + j)),
            pl.BlockSpec((tm, tc), lambda i, j: (i, j)),
            pl.BlockSpec((CONF_PAD, tc), lambda i, j: (jnp.maximum(i * halo_blocks - 1, 0), j)),
            pl.BlockSpec((None, CONF_KERNEL, tc), lambda i, j: (layer, 0, j)),
            pl.BlockSpec((None, 1, tc), lambda i, j: (layer, 0, j)),
        ],
        args=[h, w, u, u, conv_w, conv_b],
        out_specs=[pl.BlockSpec((tm, tn), lambda i, j: (i, j)),
                   pl.BlockSpec((tm, tc), lambda i, j: (i, j))],
        out_shapes=[jax.ShapeDtypeStruct((s, n_out), BF16), jax.ShapeDtypeStruct((s, dc), BF16)],
        scratch=[pltpu.VMEM((tc // LANES, CONF_PAD + tm, LANES), F32)],
        casts=casts, name="proj_sigmoid_conv31")
    return gates, cconv, cast_out


def _proj_mul_conv_ln_kernel(h_ref, w1_ref, w2_ref, cw_ref, cc_ref, lng_ref, lnb_ref, o_ref, ua_ref, e_ref, halo_ref):
    tm, tn = o_ref.shape
    ncb = tn // LANES
    pad = SHORT_PAD
    _load_halo(e_ref, halo_ref, pad)
    for r0 in range(0, tm, ROW_CHUNK):
        hh = h_ref[r0:r0 + ROW_CHUNK, :]
        _put_rows(e_ref, _dot(hh, w1_ref[...]) * _dot(hh, w2_ref[...]), 0, pad + r0)
        for c in range(ncb):
            y = _conv_slab(e_ref, cw_ref, c, c, pad + r0, ROW_CHUNK, SCONV_KERNEL)
            o_ref[r0:r0 + ROW_CHUNK, c * LANES:(c + 1) * LANES] = y.astype(o_ref.dtype)
    _save_halo(e_ref, halo_ref, pad, tm)
    for r0 in range(0, ua_ref.shape[0], LN_ROWS):
        cc = cc_ref[r0:r0 + LN_ROWS, :].astype(F32)
        mu = jnp.mean(cc, axis=-1, keepdims=True)
        dev = cc - mu
        var = jnp.mean(dev * dev, axis=-1, keepdims=True)
        y = dev * lax.rsqrt(var + LN_EPS) * lng_ref[...] + lnb_ref[...]
        ua_ref[r0:r0 + LN_ROWS, :] = (y * _sigmoid(y)).astype(ua_ref.dtype)


def _proj_mul_conv_ln(h, w, blk1, blk2, n_out, conv_w, cconv, ln_g, ln_b, layer, tm=1024, tn=512):
    s, d = h.shape
    dc = cconv.shape[1]
    nj = n_out // tn
    ncb = tn // LANES
    tr = tm // nj
    (v, ua), _ = _host_call(
        _proj_mul_conv_ln_kernel, grid=(s // tm, nj),
        in_specs=[
            pl.BlockSpec((tm, d), lambda i, j: (i, 0)),
            pl.BlockSpec((d, tn), lambda i, j: (0, blk1 + j)),
            pl.BlockSpec((d, tn), lambda i, j: (0, blk2 + j)),
            pl.BlockSpec((None, SCONV_KERNEL, tn), lambda i, j: (layer, 0, j)),
            pl.BlockSpec((tr, dc), lambda i, j: (i * nj + j, 0)),
            pl.BlockSpec((None, 1, dc), lambda i, j: (layer, 0, 0)),
            pl.BlockSpec((None, 1, dc), lambda i, j: (layer, 0, 0)),
        ],
        args=[h, w, w, conv_w, cconv, ln_g, ln_b],
        out_specs=[pl.BlockSpec((tm, tn), lambda i, j: (i, j)),
                   pl.BlockSpec((tr, dc), lambda i, j: (i * nj + j, 0))],
        out_shapes=[jax.ShapeDtypeStruct((s, n_out), BF16), jax.ShapeDtypeStruct((s, dc), BF16)],
        scratch=[pltpu.VMEM((ncb, SHORT_PAD + tm, LANES), F32),
                 pltpu.VMEM((nj, ncb, SHORT_PAD, LANES), F32)],
        name="proj_mul_conv3_ln")
    return v, ua


def _proj_mul_kernel(h_ref, w_ref, v_ref, o_ref):
    tm = o_ref.shape[0]
    for r0 in range(0, tm, ROW_CHUNK):
        t = _dot(h_ref[r0:r0 + ROW_CHUNK, :], w_ref[...])
        o_ref[r0:r0 + ROW_CHUNK, :] = (t * v_ref[r0:r0 + ROW_CHUNK, :].astype(F32)).astype(o_ref.dtype)


def _proj_mul(h, w, blk, n_out, v, tm=1024, tn=1024):
    s, d = h.shape
    (out,), _ = _host_call(
        _proj_mul_kernel, grid=(s // tm, n_out // tn),
        in_specs=[
            pl.BlockSpec((tm, d), lambda i, j: (i, 0)),
            pl.BlockSpec((d, tn), lambda i, j: (0, blk + j)),
            pl.BlockSpec((tm, tn), lambda i, j: (i, j)),
        ],
        args=[h, w, v],
        out_specs=[pl.BlockSpec((tm, tn), lambda i, j: (i, j))],
        out_shapes=[jax.ShapeDtypeStruct((s, n_out), BF16)],
        name="proj_mul")
    return out


def _mixer_kernel(ua_ref, vb_ref, ga_ref, gb_ref, wa_ref, wb_ref, o_ref):
    tm = o_ref.shape[0]
    for r0 in range(0, tm, ROW_CHUNK):
        ya = _dot(ua_ref[r0:r0 + ROW_CHUNK, :], wa_ref[...])
        yb = _dot(vb_ref[r0:r0 + ROW_CHUNK, :], wb_ref[...])
        m = ga_ref[r0:r0 + ROW_CHUNK, :].astype(F32) * ya + gb_ref[r0:r0 + ROW_CHUNK, :].astype(F32) * yb
        o_ref[r0:r0 + ROW_CHUNK, :] = m.astype(o_ref.dtype)


def _mixer(ua, vb, gates, w_a, w_b, casts, tm=1024, tn=512):
    s, dc = ua.shape
    d = w_a.shape[1]
    nj = d // tn
    (m,), cast_out = _host_call(
        _mixer_kernel, grid=(s // tm, nj),
        in_specs=[
            pl.BlockSpec((tm, dc), lambda i, j: (i, 0)),
            pl.BlockSpec((tm, dc), lambda i, j: (i, 0)),
            pl.BlockSpec((tm, tn), lambda i, j: (i, j)),
            pl.BlockSpec((tm, tn), lambda i, j: (i, nj + j)),
            pl.BlockSpec((dc, tn), lambda i, j: (0, j)),
            pl.BlockSpec((dc, tn), lambda i, j: (0, j)),
        ],
        args=[ua, vb, gates, gates, w_a, w_b],
        out_specs=[pl.BlockSpec((tm, tn), lambda i, j: (i, j))],
        out_shapes=[jax.ShapeDtypeStruct((s, d), BF16)],
        casts=casts, name="mixer")
    return m, cast_out


def _resid_kernel(a_ref, w_ref, x_ref, gt_ref, o_ref):
    tm = o_ref.shape[0]
    for r0 in range(0, tm, ROW_CHUNK):
        t = _dot(a_ref[r0:r0 + ROW_CHUNK, :], w_ref[...])
        o_ref[r0:r0 + ROW_CHUNK, :] = x_ref[r0:r0 + ROW_CHUNK, :] + gt_ref[...] * t


def _resid_proj(a, w, x, gt, tm, tn):
    s, k = a.shape
    d = x.shape[1]
    (out,), _ = _host_call(
        _resid_kernel, grid=(s // tm, d // tn),
        in_specs=[
            pl.BlockSpec((tm, k), lambda i, j: (i, 0)),
            pl.BlockSpec((k, tn), lambda i, j: (0, j)),
            pl.BlockSpec((tm, tn), lambda i, j: (i, j)),
            pl.BlockSpec((1, tn), lambda i, j: (0, j)),
        ],
        args=[a, w, x, gt],
        out_specs=[pl.BlockSpec((tm, tn), lambda i, j: (i, j))],
        out_shapes=[jax.ShapeDtypeStruct((s, d), F32)],
        name="resid_proj")
    return out


def _ffn_up_kernel(h_ref, wg_ref, wv_ref, cwg_ref, cwv_ref, o_ref, e_ref, halo_ref):
    tm, tn = o_ref.shape
    ncb = tn // LANES
    pad = SHORT_PAD
    _load_halo(e_ref, halo_ref, pad)
    for r0 in range(0, tm, ROW_CHUNK):
        hh = h_ref[r0:r0 + ROW_CHUNK, :]
        _put_rows(e_ref, _dot(hh, wg_ref[...]), 0, pad + r0)
        _put_rows(e_ref, _dot(hh, wv_ref[...]), ncb, pad + r0)
        for c in range(ncb):
            fg = _conv_slab(e_ref, cwg_ref, c, c, pad + r0, ROW_CHUNK, FFN_KERNEL)
            fv = _conv_slab(e_ref, cwv_ref, ncb + c, c, pad + r0, ROW_CHUNK, FFN_KERNEL)
            o_ref[r0:r0 + ROW_CHUNK, c * LANES:(c + 1) * LANES] = (fg * _sigmoid(fg) * fv).astype(o_ref.dtype)
    _save_halo(e_ref, halo_ref, pad, tm)


def _ffn_up(h, wg, wv, conv_w, layer, casts, tm=1024, tn=256):
    s, d = h.shape
    d_ff = wg.shape[1]
    nj = d_ff // tn
    ncb = tn // LANES
    (g,), cast_out = _host_call(
        _ffn_up_kernel, grid=(s // tm, nj),
        in_specs=[
            pl.BlockSpec((tm, d), lambda i, j: (i, 0)),
            pl.BlockSpec((d, tn), lambda i, j: (0, j)),
            pl.BlockSpec((d, tn), lambda i, j: (0, j)),
            pl.BlockSpec((None, FFN_KERNEL, tn), lambda i, j: (layer, 0, j)),
            pl.BlockSpec((None, FFN_KERNEL, tn), lambda i, j: (layer, 0, nj + j)),
        ],
        args=[h, wg, wv, conv_w, conv_w],
        out_specs=[pl.BlockSpec((tm, tn), lambda i, j: (i, j))],
        out_shapes=[jax.ShapeDtypeStruct((s, d_ff), BF16)],
        scratch=[pltpu.VMEM((2 * ncb, SHORT_PAD + tm, LANES), F32),
                 pltpu.VMEM((nj, 2 * ncb, SHORT_PAD, LANES), F32)],
        casts=casts, name="ffn_up")
    return g, cast_out


def kernel(x, c, w_ada, b_ada, g_mix, w_in, conf_w, conf_b, ln_g, ln_b, w_a_out, sconv_w, w_b_out, w_o, g_ffn, w_up, ffn_conv_w, w_down, g_final):
    batch, seq, d = x.shape
    assert batch == 1
    n_layers = w_ada.shape[0]
    d_conf = conf_w.shape[2]
    d_ff = w_down.shape[1]
    n_in = w_in.shape[2]
    nb = d_conf // 512

    mod = _ada_mod(c, w_ada, b_ada).reshape(n_layers, 6, 1, d)
    conf_b3 = conf_b.reshape(n_layers, 1, d_conf)
    ln_g3 = ln_g.reshape(n_layers, 1, d_conf)
    ln_b3 = ln_b.reshape(n_layers, 1, d_conf)

    xs = x.reshape(seq, d)
    w_in_b = w_in[0].astype(BF16)
    for l in range(n_layers):
        sh1, sc1, gt1, sh2, sc2, gt2 = (mod[l, k] for k in range(6))

        h = _norm_mod(xs, g_mix[l].reshape(1, d), sc1, sh1)
        u, (w_a_b, w_b_b, w_o_b) = _proj_glu(
            h, w_in_b, 0, nb, d_conf,
            casts=[CastJob(w_a_out, l, 0, d), CastJob(w_b_out, l, 0, d), CastJob(w_o, l, 0, d)])
        gates, cconv, (wg_b,) = _proj_sigmoid_conv(
            h, w_in_b, 5 * d_conf // 1024, 2 * d, u, conf_w, conf_b3, l,
            casts=[CastJob(w_up, l, 0, d_ff)])
        v, ua = _proj_mul_conv_ln(h, w_in_b, 3 * nb, 4 * nb, d_conf, sconv_w, cconv, ln_g3, ln_b3, l)
        vb = _proj_mul(h, w_in_b, 2 * d_conf // 1024, d_conf, v)
        m, (wv_b,) = _mixer(ua, vb, gates, w_a_b, w_b_b, casts=[CastJob(w_up, l, 1, d_ff)])
        xs = _resid_proj(m, w_o_b, xs, gt1, tm=1024, tn=1024)

        h2 = _norm_mod(xs, g_ffn[l].reshape(1, d), sc2, sh2)
        casts = [CastJob(w_down, l, 0, d)]
        if l + 1 < n_layers:
            casts.append(CastJob(w_in, l + 1, 0, n_in))
        g, cast_out = _ffn_up(h2, wg_b, wv_b, ffn_conv_w, l, casts)
        w_down_b = cast_out[0]
        if l + 1 < n_layers:
            w_in_b = cast_out[1]
        xs = _resid_proj(g, w_down_b, xs, gt2, tm=512, tn=512)

    return _final_norm(xs, g_final.reshape(1, d)).reshape(batch, seq, d)
```

```python
import collections
import functools

import jax
import jax.numpy as jnp
from jax import lax
from jax.experimental import pallas as pl
from jax.experimental.pallas import tpu as pltpu

F32 = jnp.float32
BF16 = jnp.bfloat16

RMS_EPS = 1e-6
LN_EPS = 1e-5
CONF_KERNEL = 31
SCONV_KERNEL = 3
FFN_KERNEL = 3

LANES = 128
SUBLANES = 8
VMEM_LIMIT = 56 * 1024 * 1024

ROW_CHUNK = 256
CONF_PAD = 32
SHORT_PAD = 8
NORM_ROWS = 32


def _params(sem):
    return pltpu.CompilerParams(dimension_semantics=sem, vmem_limit_bytes=VMEM_LIMIT)


def _sigmoid(x):
    return jax.nn.sigmoid(x)


def _dot(a, b):
    return jnp.dot(a, b, preferred_element_type=F32)


def _or_fold(y):
    bits = pltpu.bitcast(y, jnp.uint32)
    acc = None
    for r in range(0, bits.shape[0], SUBLANES):
        for c in range(0, bits.shape[1], LANES):
            t = bits[r:r + SUBLANES, c:c + LANES]
            acc = t if acc is None else acc | t
    return acc


def _zero_tile():
    return jnp.zeros((SUBLANES, LANES), jnp.uint32)


def _tie(lhs, fold, zero_ref):
    bits = pltpu.bitcast(lhs, jnp.uint32)
    zero = fold & zero_ref[...]
    bits = bits | jnp.tile(zero, (bits.shape[0] // SUBLANES, bits.shape[1] // LANES))
    return pltpu.bitcast(bits, lhs.dtype)


CastJob = collections.namedtuple("CastJob", "w layer col_block cols")
BF16_ROWS = 16


def _cast_rows(n_rows, n_steps):
    rows = BF16_ROWS
    while n_rows % rows or n_rows // rows > n_steps:
        rows += BF16_ROWS
    return rows


def _host_call(body, *, grid, in_specs, args, out_specs, out_shapes, scratch=(), casts=(), name):
    gi, gj = grid
    n_in, n_out, n_cast = len(args), len(out_specs), len(casts)
    in_specs, out_specs, out_shapes, args = list(in_specs), list(out_specs), list(out_shapes), list(args)
    for job in casts:
        n_rows = job.w.shape[1]
        rows = _cast_rows(n_rows, gi * gj)
        n_blocks = n_rows // rows

        def block(i, j, n_blocks=n_blocks):
            return jnp.minimum(i * gj + j, n_blocks - 1)

        in_specs.append(pl.BlockSpec((None, rows, job.cols),
                                     lambda i, j, job=job, block=block: (job.layer, block(i, j), job.col_block)))
        out_specs.append(pl.BlockSpec((rows, job.cols), lambda i, j, block=block: (block(i, j), 0)))
        out_shapes.append(jax.ShapeDtypeStruct((n_rows, job.cols), BF16))
        args.append(job.w)

    def wrapped(*refs):
        main_in = refs[:n_in]
        cast_in = refs[n_in:n_in + n_cast]
        main_out = refs[n_in + n_cast:n_in + n_cast + n_out]
        cast_out = refs[n_in + n_cast + n_out:n_in + 2 * n_cast + n_out]
        rest = refs[n_in + 2 * n_cast + n_out:]
        for src, dst in zip(cast_in, cast_out):
            dst[...] = src[...].astype(BF16)
        body(*main_in, *main_out, *rest)

    outs = pl.pallas_call(
        wrapped, grid=grid, in_specs=in_specs, out_specs=out_specs, out_shape=out_shapes,
        scratch_shapes=list(scratch), compiler_params=_params(("arbitrary", "arbitrary")), name=name,
    )(*args)
    return outs[:n_out], outs[n_out:]


def _ada_kernel(c_ref, w_ref, b_ref, o_ref):
    c = c_ref[...]
    ca = (c * _sigmoid(c)).astype(BF16)
    o_ref[...] = _dot(ca, w_ref[...].astype(BF16)) + b_ref[...]


def _ada_mod(c, w_ada, b_ada):
    n_layers, d, n = w_ada.shape
    tn = 1024
    c8 = jnp.broadcast_to(c.reshape(1, d), (SUBLANES, d))
    out = pl.pallas_call(
        _ada_kernel,
        grid=(n_layers, n // tn),
        in_specs=[
            pl.BlockSpec((SUBLANES, d), lambda l, j: (0, 0)),
            pl.BlockSpec((None, d, tn), lambda l, j: (l, 0, j)),
            pl.BlockSpec((None, 1, tn), lambda l, j: (l, 0, j)),
        ],
        out_specs=pl.BlockSpec((None, SUBLANES, tn), lambda l, j: (l, 0, j)),
        out_shape=jax.ShapeDtypeStruct((n_layers, SUBLANES, n), F32),
        compiler_params=_params(("arbitrary", "arbitrary")),
        name="ada_mod",
    )(c8, w_ada, b_ada.reshape(n_layers, 1, n))
    return out[:, 0, :]


def _norm_mod_kernel(x_ref, g_ref, sc_ref, sh_ref, o_ref):
    for r0 in range(0, o_ref.shape[0], NORM_ROWS):
        x = x_ref[r0:r0 + NORM_ROWS, :]
        ms = jnp.mean(x * x, axis=-1, keepdims=True)
        y = x * lax.rsqrt(ms + RMS_EPS) * g_ref[...]
        o_ref[r0:r0 + NORM_ROWS, :] = (y * (1.0 + sc_ref[...]) + sh_ref[...]).astype(o_ref.dtype)


def _norm_kernel(x_ref, g_ref, o_ref):
    for r0 in range(0, o_ref.shape[0], NORM_ROWS):
        x = x_ref[r0:r0 + NORM_ROWS, :]
        ms = jnp.mean(x * x, axis=-1, keepdims=True)
        o_ref[r0:r0 + NORM_ROWS, :] = (x * lax.rsqrt(ms + RMS_EPS) * g_ref[...]).astype(o_ref.dtype)


def _norm_mod(x, g, sc, sh):
    s, d = x.shape
    tr = 512
    row = pl.BlockSpec((tr, d), lambda i: (i, 0))
    vec = pl.BlockSpec((1, d), lambda i: (0, 0))
    return pl.pallas_call(
        _norm_mod_kernel, grid=(s // tr,),
        in_specs=[row, vec, vec, vec], out_specs=row,
        out_shape=jax.ShapeDtypeStruct((s, d), BF16),
        compiler_params=_params(("arbitrary",)), name="norm_mod",
    )(x, g, sc, sh)


def _final_norm(x, g):
    s, d = x.shape
    tr = 512
    row = pl.BlockSpec((tr, d), lambda i: (i, 0))
    vec = pl.BlockSpec((1, d), lambda i: (0, 0))
    return pl.pallas_call(
        _norm_kernel, grid=(s // tr,),
        in_specs=[row, vec], out_specs=row,
        out_shape=jax.ShapeDtypeStruct((s, d), F32),
        compiler_params=_params(("arbitrary",)), name="final_norm",
    )(x, g)


def _load_halo(e_ref, halo_ref, pad):
    i = pl.program_id(0)
    j = pl.program_id(1)

    @pl.when(i == 0)
    def _():
        e_ref[:, 0:pad, :] = jnp.zeros((e_ref.shape[0], pad, LANES), F32)

    @pl.when(i > 0)
    def _():
        e_ref[:, 0:pad, :] = halo_ref[j]


def _save_halo(e_ref, halo_ref, pad, tm):
    halo_ref[pl.program_id(1)] = e_ref[:, tm:tm + pad, :]


def _put_rows(e_ref, u, slab0, row0):
    for c in range(u.shape[1] // LANES):
        e_ref[slab0 + c, row0:row0 + u.shape[0], :] = u[:, c * LANES:(c + 1) * LANES]


def _conv_slab(e_ref, cw_ref, slab, col, row0, rows, k_taps):
    acc = None
    for kk in range(k_taps):
        shift = k_taps - 1 - kk
        term = e_ref[slab, row0 - shift:row0 - shift + rows, :] * cw_ref[kk:kk + 1, col * LANES:(col + 1) * LANES]
        acc = term if acc is None else acc + term
    return acc


def _proj_glu_kernel(h_ref, w1_ref, w2_ref, o_ref):
    tm = o_ref.shape[0]
    for r0 in range(0, tm, ROW_CHUNK):
        hh = h_ref[r0:r0 + ROW_CHUNK, :]
        t1 = _dot(hh, w1_ref[...])
        t2 = _dot(hh, w2_ref[...])
        o_ref[r0:r0 + ROW_CHUNK, :] = (t1 * _sigmoid(t2)).astype(o_ref.dtype)


def _proj_glu(h, w, blk1, blk2, n_out, casts, tm=1024, tn=512):
    s, d = h.shape
    (u,), cast_out = _host_call(
        _proj_glu_kernel, grid=(s // tm, n_out // tn),
        in_specs=[
            pl.BlockSpec((tm, d), lambda i, j: (i, 0)),
            pl.BlockSpec((d, tn), lambda i, j: (0, blk1 + j)),
            pl.BlockSpec((d, tn), lambda i, j: (0, blk2 + j)),
        ],
        args=[h, w, w],
        out_specs=[pl.BlockSpec((tm, tn), lambda i, j: (i, j))],
        out_shapes=[jax.ShapeDtypeStruct((s, n_out), BF16)],
        casts=casts, name="proj_glu")
    return u, cast_out


def _proj_sigmoid_conv_kernel(h_ref, w_ref, um_ref, uh_ref, cw_ref, cb_ref, z_ref, o_ref, c_ref, e_ref):
    tm, tn = o_ref.shape
    half = tn // 2
    ncb = c_ref.shape[1] // LANES
    pad = CONF_PAD
    first = pl.program_id(0) == 0
    _put_rows(e_ref, jnp.where(first, 0.0, uh_ref[...].astype(F32)), 0, 0)
    for r0 in range(0, tm, ROW_CHUNK):
        _put_rows(e_ref, um_ref[r0:r0 + ROW_CHUNK, :].astype(F32), 0, pad + r0)
    for r0 in range(0, tm, ROW_CHUNK):
        hh = h_ref[r0:r0 + ROW_CHUNK, :]
        t = _dot(hh, w_ref[:, 0:half])
        o_ref[r0:r0 + ROW_CHUNK, 0:half] = _sigmoid(t).astype(o_ref.dtype)
        fold = None
        for c in range(ncb):
            y = _conv_slab(e_ref, cw_ref, c, c, pad + r0, ROW_CHUNK, CONF_KERNEL)
            y = y + cb_ref[:, c * LANES:(c + 1) * LANES]
            c_ref[r0:r0 + ROW_CHUNK, c * LANES:(c + 1) * LANES] = y.astype(c_ref.dtype)
            f = _or_fold(y)
            fold = f if fold is None else fold | f
        t = _dot(_tie(hh, fold, z_ref), w_ref[:, half:tn])
        o_ref[r0:r0 + ROW_CHUNK, half:tn] = _sigmoid(t).astype(o_ref.dtype)


def _proj_sigmoid_conv(h, w, blk, n_out, u, conv_w, conv_b, layer, casts, tm=1024, tn=1024):
    s, d = h.shape
    dc = u.shape[1]
    nj = n_out // tn
    tc = dc // nj
    halo_blocks = tm // CONF_PAD
    (gates, cconv), cast_out = _host_call(
        _proj_sigmoid_conv_kernel, grid=(s // tm, nj),
        in_specs=[
            pl.BlockSpec((tm, d), lambda i, j: (i, 0)),
            pl.BlockSpec((d, tn), lambda i, j: (0, blk + j)),
            pl.BlockSpec((tm, tc), lambda i, j: (i, j)),
            pl.BlockSpec((CONF_PAD, tc), lambda i, j: (jnp.maximum(i * halo_blocks - 1, 0), j)),
            pl.BlockSpec((None, CONF_KERNEL, tc), lambda i, j: (layer, 0, j)),
            pl.BlockSpec((None, 1, tc), lambda i, j: (layer, 0, j)),
            pl.BlockSpec((SUBLANES, LANES), lambda i, j: (0, 0)),
        ],
        args=[h, w, u, u, conv_w, conv_b, _zero_tile()],
        out_specs=[pl.BlockSpec((tm, tn), lambda i, j: (i, j)),
                   pl.BlockSpec((tm, tc), lambda i, j: (i, j))],
        out_shapes=[jax.ShapeDtypeStruct((s, n_out), BF16), jax.ShapeDtypeStruct((s, dc), BF16)],
        scratch=[pltpu.VMEM((tc // LANES, CONF_PAD + tm, LANES), F32)],
        casts=casts, name="proj_sigmoid_conv31")
    return gates, cconv, cast_out


def _ln_swish_piece(cc_ref, lng_ref, lnb_ref, ua_ref, q0, rows):
    cc = cc_ref[q0:q0 + rows, :].astype(F32)
    mu = jnp.mean(cc, axis=-1, keepdims=True)
    dev = cc - mu
    var = jnp.mean(dev * dev, axis=-1, keepdims=True)
    y = dev * lax.rsqrt(var + LN_EPS) * lng_ref[...] + lnb_ref[...]
    y = y * _sigmoid(y)
    ua_ref[q0:q0 + rows, :] = y.astype(ua_ref.dtype)
    return y


def _proj_mul_conv_ln_kernel(h_ref, w1_ref, w2_ref, cw_ref, cc_ref, lng_ref, lnb_ref, z_ref, o_ref, ua_ref, e_ref, halo_ref):
    tm, tn = o_ref.shape
    ncb = tn // LANES
    pad = SHORT_PAD
    n_chunks = tm // ROW_CHUNK
    ln_rows = ua_ref.shape[0] // n_chunks
    _load_halo(e_ref, halo_ref, pad)
    for k in range(n_chunks):
        r0 = k * ROW_CHUNK
        hh = h_ref[r0:r0 + ROW_CHUNK, :]
        t1 = _dot(hh, w1_ref[...])
        y = _ln_swish_piece(cc_ref, lng_ref, lnb_ref, ua_ref, k * ln_rows, ln_rows)
        t2 = _dot(_tie(hh, _or_fold(y), z_ref), w2_ref[...])
        _put_rows(e_ref, t1 * t2, 0, pad + r0)
        for c in range(ncb):
            v = _conv_slab(e_ref, cw_ref, c, c, pad + r0, ROW_CHUNK, SCONV_KERNEL)
            o_ref[r0:r0 + ROW_CHUNK, c * LANES:(c + 1) * LANES] = v.astype(o_ref.dtype)
    _save_halo(e_ref, halo_ref, pad, tm)


def _proj_mul_conv_ln(h, w, blk1, blk2, n_out, conv_w, cconv, ln_g, ln_b, layer, tm=1024, tn=512):
    s, d = h.shape
    dc = cconv.shape[1]
    nj = n_out // tn
    ncb = tn // LANES
    tr = tm // nj
    (v, ua), _ = _host_call(
        _proj_mul_conv_ln_kernel, grid=(s // tm, nj),
        in_specs=[
            pl.BlockSpec((tm, d), lambda i, j: (i, 0)),
            pl.BlockSpec((d, tn), lambda i, j: (0, blk1 + j)),
            pl.BlockSpec((d, tn), lambda i, j: (0, blk2 + j)),
            pl.BlockSpec((None, SCONV_KERNEL, tn), lambda i, j: (layer, 0, j)),
            pl.BlockSpec((tr, dc), lambda i, j: (i * nj + j, 0)),
            pl.BlockSpec((None, 1, dc), lambda i, j: (layer, 0, 0)),
            pl.BlockSpec((None, 1, dc), lambda i, j: (layer, 0, 0)),
            pl.BlockSpec((SUBLANES, LANES), lambda i, j: (0, 0)),
        ],
        args=[h, w, w, conv_w, cconv, ln_g, ln_b, _zero_tile()],
        out_specs=[pl.BlockSpec((tm, tn), lambda i, j: (i, j)),
                   pl.BlockSpec((tr, dc), lambda i, j: (i * nj + j, 0))],
        out_shapes=[jax.ShapeDtypeStruct((s, n_out), BF16), jax.ShapeDtypeStruct((s, dc), BF16)],
        scratch=[pltpu.VMEM((ncb, SHORT_PAD + tm, LANES), F32),
                 pltpu.VMEM((nj, ncb, SHORT_PAD, LANES), F32)],
        name="proj_mul_conv3_ln")
    return v, ua


def _proj_mul_kernel(h_ref, w_ref, v_ref, o_ref):
    tm = o_ref.shape[0]
    for r0 in range(0, tm, ROW_CHUNK):
        t = _dot(h_ref[r0:r0 + ROW_CHUNK, :], w_ref[...])
        o_ref[r0:r0 + ROW_CHUNK, :] = (t * v_ref[r0:r0 + ROW_CHUNK, :].astype(F32)).astype(o_ref.dtype)


def _proj_mul(h, w, blk, n_out, v, tm=1024, tn=1024):
    s, d = h.shape
    (out,), _ = _host_call(
        _proj_mul_kernel, grid=(s // tm, n_out // tn),
        in_specs=[
            pl.BlockSpec((tm, d), lambda i, j: (i, 0)),
            pl.BlockSpec((d, tn), lambda i, j: (0, blk + j)),
            pl.BlockSpec((tm, tn), lambda i, j: (i, j)),
        ],
        args=[h, w, v],
        out_specs=[pl.BlockSpec((tm, tn), lambda i, j: (i, j))],
        out_shapes=[jax.ShapeDtypeStruct((s, n_out), BF16)],
        name="proj_mul")
    return out


def _mixer_kernel(ua_ref, vb_ref, ga_ref, gb_ref, wa_ref, wb_ref, o_ref):
    tm = o_ref.shape[0]
    for r0 in range(0, tm, ROW_CHUNK):
        ya = _dot(ua_ref[r0:r0 + ROW_CHUNK, :], wa_ref[...])
        yb = _dot(vb_ref[r0:r0 + ROW_CHUNK, :], wb_ref[...])
        m = ga_ref[r0:r0 + ROW_CHUNK, :].astype(F32) * ya + gb_ref[r0:r0 + ROW_CHUNK, :].astype(F32) * yb
        o_ref[r0:r0 + ROW_CHUNK, :] = m.astype(o_ref.dtype)


def _mixer(ua, vb, gates, w_a, w_b, casts, tm=1024, tn=512):
    s, dc = ua.shape
    d = w_a.shape[1]
    nj = d // tn
    (m,), cast_out = _host_call(
        _mixer_kernel, grid=(s // tm, nj),
        in_specs=[
            pl.BlockSpec((tm, dc), lambda i, j: (i, 0)),
            pl.BlockSpec((tm, dc), lambda i, j: (i, 0)),
            pl.BlockSpec((tm, tn), lambda i, j: (i, j)),
            pl.BlockSpec((tm, tn), lambda i, j: (i, nj + j)),
            pl.BlockSpec((dc, tn), lambda i, j: (0, j)),
            pl.BlockSpec((dc, tn), lambda i, j: (0, j)),
        ],
        args=[ua, vb, gates, gates, w_a, w_b],
        out_specs=[pl.BlockSpec((tm, tn), lambda i, j: (i, j))],
        out_shapes=[jax.ShapeDtypeStruct((s, d), BF16)],
        casts=casts, name="mixer")
    return m, cast_out


def _resid_kernel(a_ref, w_ref, x_ref, gt_ref, o_ref):
    tm = o_ref.shape[0]
    for r0 in range(0, tm, ROW_CHUNK):
        t = _dot(a_ref[r0:r0 + ROW_CHUNK, :], w_ref[...])
        o_ref[r0:r0 + ROW_CHUNK, :] = x_ref[r0:r0 + ROW_CHUNK, :] + gt_ref[...] * t


def _resid_proj(a, w, x, gt, tm, tn):
    s, k = a.shape
    d = x.shape[1]
    (out,), _ = _host_call(
        _resid_kernel, grid=(s // tm, d // tn),
        in_specs=[
            pl.BlockSpec((tm, k), lambda i, j: (i, 0)),
            pl.BlockSpec((k, tn), lambda i, j: (0, j)),
            pl.BlockSpec((tm, tn), lambda i, j: (i, j)),
            pl.BlockSpec((1, tn), lambda i, j: (0, j)),
        ],
        args=[a, w, x, gt],
        out_specs=[pl.BlockSpec((tm, tn), lambda i, j: (i, j))],
        out_shapes=[jax.ShapeDtypeStruct((s, d), F32)],
        name="resid_proj")
    return out


def _ffn_up_kernel(h_ref, wg_ref, wv_ref, cwg_ref, cwv_ref, o_ref, e_ref, halo_ref):
    tm, tn = o_ref.shape
    ncb = tn // LANES
    pad = SHORT_PAD
    _load_halo(e_ref, halo_ref, pad)
    for r0 in range(0, tm, ROW_CHUNK):
        hh = h_ref[r0:r0 + ROW_CHUNK, :]
        _put_rows(e_ref, _dot(hh, wg_ref[...]), 0, pad + r0)
        _put_rows(e_ref, _dot(hh, wv_ref[...]), ncb, pad + r0)
        for c in range(ncb):
            fg = _conv_slab(e_ref, cwg_ref, c, c, pad + r0, ROW_CHUNK, FFN_KERNEL)
            fv = _conv_slab(e_ref, cwv_ref, ncb + c, c, pad + r0, ROW_CHUNK, FFN_KERNEL)
            o_ref[r0:r0 + ROW_CHUNK, c * LANES:(c + 1) * LANES] = (fg * _sigmoid(fg) * fv).astype(o_ref.dtype)
    _save_halo(e_ref, halo_ref, pad, tm)


def _ffn_up(h, wg, wv, conv_w, layer, casts, tm=2048, tn=256):
    s, d = h.shape
    d_ff = wg.shape[1]
    nj = d_ff // tn
    ncb = tn // LANES
    (g,), cast_out = _host_call(
        _ffn_up_kernel, grid=(s // tm, nj),
        in_specs=[
            pl.BlockSpec((tm, d), lambda i, j: (i, 0), pipeline_mode=pl.Buffered(1)),
            pl.BlockSpec((d, tn), lambda i, j: (0, j)),
            pl.BlockSpec((d, tn), lambda i, j: (0, j)),
            pl.BlockSpec((None, FFN_KERNEL, tn), lambda i, j: (layer, 0, j)),
            pl.BlockSpec((None, FFN_KERNEL, tn), lambda i, j: (layer, 0, nj + j)),
        ],
        args=[h, wg, wv, conv_w, conv_w],
        out_specs=[pl.BlockSpec((tm, tn), lambda i, j: (i, j))],
        out_shapes=[jax.ShapeDtypeStruct((s, d_ff), BF16)],
        scratch=[pltpu.VMEM((2 * ncb, SHORT_PAD + tm, LANES), F32),
                 pltpu.VMEM((nj, 2 * ncb, SHORT_PAD, LANES), F32)],
        casts=casts, name="ffn_up")
    return g, cast_out


def kernel(x, c, w_ada, b_ada, g_mix, w_in, conf_w, conf_b, ln_g, ln_b, w_a_out, sconv_w, w_b_out, w_o, g_ffn, w_up, ffn_conv_w, w_down, g_final):
    batch, seq, d = x.shape
    assert batch == 1
    n_layers = w_ada.shape[0]
    d_conf = conf_w.shape[2]
    d_ff = w_down.shape[1]
    n_in = w_in.shape[2]
    nb = d_conf // 512

    mod = _ada_mod(c, w_ada, b_ada).reshape(n_layers, 6, 1, d)
    conf_b3 = conf_b.reshape(n_layers, 1, d_conf)
    ln_g3 = ln_g.reshape(n_layers, 1, d_conf)
    ln_b3 = ln_b.reshape(n_layers, 1, d_conf)

    xs = x.reshape(seq, d)
    w_in_b = w_in[0].astype(BF16)
    for l in range(n_layers):
        sh1, sc1, gt1, sh2, sc2, gt2 = (mod[l, k] for k in range(6))

        h = _norm_mod(xs, g_mix[l].reshape(1, d), sc1, sh1)
        u, (w_a_b, w_b_b, w_o_b) = _proj_glu(
            h, w_in_b, 0, nb, d_conf,
            casts=[CastJob(w_a_out, l, 0, d), CastJob(w_b_out, l, 0, d), CastJob(w_o, l, 0, d)])
        gates, cconv, (wg_b,) = _proj_sigmoid_conv(
            h, w_in_b, 5 * d_conf // 1024, 2 * d, u, conf_w, conf_b3, l,
            casts=[CastJob(w_up, l, 0, d_ff)])
        v, ua = _proj_mul_conv_ln(h, w_in_b, 3 * nb, 4 * nb, d_conf, sconv_w, cconv, ln_g3, ln_b3, l)
        vb = _proj_mul(h, w_in_b, 2 * d_conf // 1024, d_conf, v)
        m, (wv_b,) = _mixer(ua, vb, gates, w_a_b, w_b_b, casts=[CastJob(w_up, l, 1, d_ff)])
        xs = _resid_proj(m, w_o_b, xs, gt1, tm=1024, tn=1024)

        h2 = _norm_mod(xs, g_ffn[l].reshape(1, d), sc2, sh2)
        casts = [CastJob(w_down, l, 0, d)]
        if l + 1 < n_layers:
            casts.append(CastJob(w_in, l + 1, 0, n_in))
        g, cast_out = _ffn_up(h2, wg_b, wv_b, ffn_conv_w, l, casts)
        w_down_b = cast_out[0]
        if l + 1 < n_layers:
            w_in_b = cast_out[1]
        xs = _resid_proj(g, w_down_b, xs, gt2, tm=512, tn=512)

    return _final_norm(xs, g_final.reshape(1, d)).reshape(batch, seq, d)
```

```python
import collections
import functools

import jax
import jax.numpy as jnp
from jax import lax
from jax.experimental import pallas as pl
from jax.experimental.pallas import tpu as pltpu

F32 = jnp.float32
BF16 = jnp.bfloat16

RMS_EPS = 1e-6
LN_EPS = 1e-5
CONF_KERNEL = 31
SCONV_KERNEL = 3
FFN_KERNEL = 3

LANES = 128
SUBLANES = 8
VMEM_LIMIT = 56 * 1024 * 1024

ROW_CHUNK = 256
CONF_PAD = 32
SHORT_PAD = 8
NORM_ROWS = 32


def _params(sem):
    return pltpu.CompilerParams(dimension_semantics=sem, vmem_limit_bytes=VMEM_LIMIT)


def _sigmoid(x):
    return jax.nn.sigmoid(x)


def _dot(a, b):
    return jnp.dot(a, b, preferred_element_type=F32)


def _or_fold(y):
    bits = pltpu.bitcast(y, jnp.uint32)
    acc = None
    for r in range(0, bits.shape[0], SUBLANES):
        for c in range(0, bits.shape[1], LANES):
            t = bits[r:r + SUBLANES, c:c + LANES]
            acc = t if acc is None else acc | t
    return acc


def _zero_tile():
    return jnp.zeros((SUBLANES, LANES), jnp.uint32)


def _tie(lhs, fold, zero_ref):
    bits = pltpu.bitcast(lhs, jnp.uint32)
    zero = fold & zero_ref[...]
    bits = bits | jnp.tile(zero, (bits.shape[0] // SUBLANES, bits.shape[1] // LANES))
    return pltpu.bitcast(bits, lhs.dtype)


CastJob = collections.namedtuple("CastJob", "w layer col_block cols")
BF16_ROWS = 16


def _cast_rows(n_rows, n_steps):
    rows = BF16_ROWS
    while n_rows % rows or n_rows // rows > n_steps:
        rows += BF16_ROWS
    return rows


def _host_call(body, *, grid, in_specs, args, out_specs, out_shapes, scratch=(), casts=(), name):
    gi, gj = grid
    n_in, n_out, n_cast = len(args), len(out_specs), len(casts)
    in_specs, out_specs, out_shapes, args = list(in_specs), list(out_specs), list(out_shapes), list(args)
    for job in casts:
        n_rows = job.w.shape[1]
        rows = _cast_rows(n_rows, gi * gj)
        n_blocks = n_rows // rows

        def block(i, j, n_blocks=n_blocks):
            return jnp.minimum(i * gj + j, n_blocks - 1)

        in_specs.append(pl.BlockSpec((None, rows, job.cols),
                                     lambda i, j, job=job, block=block: (job.layer, block(i, j), job.col_block)))
        out_specs.append(pl.BlockSpec((rows, job.cols), lambda i, j, block=block: (block(i, j), 0)))
        out_shapes.append(jax.ShapeDtypeStruct((n_rows, job.cols), BF16))
        args.append(job.w)

    def wrapped(*refs):
        main_in = refs[:n_in]
        cast_in = refs[n_in:n_in + n_cast]
        main_out = refs[n_in + n_cast:n_in + n_cast + n_out]
        cast_out = refs[n_in + n_cast + n_out:n_in + 2 * n_cast + n_out]
        rest = refs[n_in + 2 * n_cast + n_out:]
        for src, dst in zip(cast_in, cast_out):
            dst[...] = src[...].astype(BF16)
        body(*main_in, *main_out, *rest)

    outs = pl.pallas_call(
        wrapped, grid=grid, in_specs=in_specs, out_specs=out_specs, out_shape=out_shapes,
        scratch_shapes=list(scratch), compiler_params=_params(("arbitrary", "arbitrary")), name=name,
    )(*args)
    return outs[:n_out], outs[n_out:]


def _ada_kernel(c_ref, w_ref, b_ref, o_ref):
    c = c_ref[...]
    ca = (c * _sigmoid(c)).astype(BF16)
    o_ref[...] = _dot(ca, w_ref[...].astype(BF16)) + b_ref[...]


def _ada_mod(c, w_ada, b_ada):
    n_layers, d, n = w_ada.shape
    tn = 1024
    c8 = jnp.broadcast_to(c.reshape(1, d), (SUBLANES, d))
    out = pl.pallas_call(
        _ada_kernel,
        grid=(n_layers, n // tn),
        in_specs=[
            pl.BlockSpec((SUBLANES, d), lambda l, j: (0, 0)),
            pl.BlockSpec((None, d, tn), lambda l, j: (l, 0, j)),
            pl.BlockSpec((None, 1, tn), lambda l, j: (l, 0, j)),
        ],
        out_specs=pl.BlockSpec((None, SUBLANES, tn), lambda l, j: (l, 0, j)),
        out_shape=jax.ShapeDtypeStruct((n_layers, SUBLANES, n), F32),
        compiler_params=_params(("arbitrary", "arbitrary")),
        name="ada_mod",
    )(c8, w_ada, b_ada.reshape(n_layers, 1, n))
    return out[:, 0, :]


def _norm_mod_kernel(x_ref, g_ref, sc_ref, sh_ref, o_ref):
    gain = g_ref[...] * (1.0 + sc_ref[...])
    for r0 in range(0, o_ref.shape[0], NORM_ROWS):
        x = x_ref[r0:r0 + NORM_ROWS, :]
        ms = jnp.mean(x * x, axis=-1, keepdims=True)
        o_ref[r0:r0 + NORM_ROWS, :] = (x * lax.rsqrt(ms + RMS_EPS) * gain + sh_ref[...]).astype(o_ref.dtype)


def _norm_kernel(x_ref, g_ref, o_ref):
    for r0 in range(0, o_ref.shape[0], NORM_ROWS):
        x = x_ref[r0:r0 + NORM_ROWS, :]
        ms = jnp.mean(x * x, axis=-1, keepdims=True)
        o_ref[r0:r0 + NORM_ROWS, :] = (x * lax.rsqrt(ms + RMS_EPS) * g_ref[...]).astype(o_ref.dtype)


def _norm_mod(x, g, sc, sh):
    s, d = x.shape
    tr = 1024
    row = pl.BlockSpec((tr, d), lambda i: (i, 0))
    vec = pl.BlockSpec((1, d), lambda i: (0, 0))
    return pl.pallas_call(
        _norm_mod_kernel, grid=(s // tr,),
        in_specs=[row, vec, vec, vec], out_specs=row,
        out_shape=jax.ShapeDtypeStruct((s, d), BF16),
        compiler_params=_params(("arbitrary",)), name="norm_mod",
    )(x, g, sc, sh)


def _final_norm(x, g):
    s, d = x.shape
    tr = 512
    row = pl.BlockSpec((tr, d), lambda i: (i, 0))
    vec = pl.BlockSpec((1, d), lambda i: (0, 0))
    return pl.pallas_call(
        _norm_kernel, grid=(s // tr,),
        in_specs=[row, vec], out_specs=row,
        out_shape=jax.ShapeDtypeStruct((s, d), F32),
        compiler_params=_params(("arbitrary",)), name="final_norm",
    )(x, g)


def _load_halo(e_ref, halo_ref, pad):
    i = pl.program_id(0)
    j = pl.program_id(1)

    @pl.when(i == 0)
    def _():
        e_ref[:, 0:pad, :] = jnp.zeros((e_ref.shape[0], pad, LANES), F32)

    @pl.when(i > 0)
    def _():
        e_ref[:, 0:pad, :] = halo_ref[j]


def _save_halo(e_ref, halo_ref, pad, tm):
    halo_ref[pl.program_id(1)] = e_ref[:, tm:tm + pad, :]


def _put_rows(e_ref, u, slab0, row0):
    for c in range(u.shape[1] // LANES):
        e_ref[slab0 + c, row0:row0 + u.shape[0], :] = u[:, c * LANES:(c + 1) * LANES]


def _conv_slab(e_ref, cw_ref, slab, col, row0, rows, k_taps):
    acc = None
    for kk in range(k_taps):
        shift = k_taps - 1 - kk
        term = e_ref[slab, row0 - shift:row0 - shift + rows, :] * cw_ref[kk:kk + 1, col * LANES:(col + 1) * LANES]
        acc = term if acc is None else acc + term
    return acc


def _proj_glu_kernel(h_ref, w1_ref, w2_ref, o_ref):
    tm = o_ref.shape[0]
    for r0 in range(0, tm, ROW_CHUNK):
        hh = h_ref[r0:r0 + ROW_CHUNK, :]
        t1 = _dot(hh, w1_ref[...])
        t2 = _dot(hh, w2_ref[...])
        o_ref[r0:r0 + ROW_CHUNK, :] = (t1 * _sigmoid(t2)).astype(o_ref.dtype)


def _proj_glu(h, w, blk1, blk2, n_out, casts, tm=1024, tn=512):
    s, d = h.shape
    (u,), cast_out = _host_call(
        _proj_glu_kernel, grid=(s // tm, n_out // tn),
        in_specs=[
            pl.BlockSpec((tm, d), lambda i, j: (i, 0)),
            pl.BlockSpec((d, tn), lambda i, j: (0, blk1 + j)),
            pl.BlockSpec((d, tn), lambda i, j: (0, blk2 + j)),
        ],
        args=[h, w, w],
        out_specs=[pl.BlockSpec((tm, tn), lambda i, j: (i, j))],
        out_shapes=[jax.ShapeDtypeStruct((s, n_out), BF16)],
        casts=casts, name="proj_glu")
    return u, cast_out


def _proj_sigmoid_conv_kernel(h_ref, w_ref, um_ref, uh_ref, cw_ref, cb_ref, z_ref, o_ref, c_ref, e_ref):
    tm, tn = o_ref.shape
    half = tn // 2
    ncb = c_ref.shape[1] // LANES
    pad = CONF_PAD
    first = pl.program_id(0) == 0
    _put_rows(e_ref, jnp.where(first, 0.0, uh_ref[...].astype(F32)), 0, 0)
    for r0 in range(0, tm, ROW_CHUNK):
        _put_rows(e_ref, um_ref[r0:r0 + ROW_CHUNK, :].astype(F32), 0, pad + r0)
    for r0 in range(0, tm, ROW_CHUNK):
        hh = h_ref[r0:r0 + ROW_CHUNK, :]
        t = _dot(hh, w_ref[:, 0:half])
        o_ref[r0:r0 + ROW_CHUNK, 0:half] = _sigmoid(t).astype(o_ref.dtype)
        fold = None
        for c in range(ncb):
            y = _conv_slab(e_ref, cw_ref, c, c, pad + r0, ROW_CHUNK, CONF_KERNEL)
            y = y + cb_ref[:, c * LANES:(c + 1) * LANES]
            c_ref[r0:r0 + ROW_CHUNK, c * LANES:(c + 1) * LANES] = y.astype(c_ref.dtype)
            f = _or_fold(y)
            fold = f if fold is None else fold | f
        t = _dot(_tie(hh, fold, z_ref), w_ref[:, half:tn])
        o_ref[r0:r0 + ROW_CHUNK, half:tn] = _sigmoid(t).astype(o_ref.dtype)


def _proj_sigmoid_conv(h, w, blk, n_out, u, conv_w, conv_b, layer, casts, tm=1024, tn=1024):
    s, d = h.shape
    dc = u.shape[1]
    nj = n_out // tn
    tc = dc // nj
    halo_blocks = tm // CONF_PAD
    (gates, cconv), cast_out = _host_call(
        _proj_sigmoid_conv_kernel, grid=(s // tm, nj),
        in_specs=[
            pl.BlockSpec((tm, d), lambda i, j: (i, 0)),
            pl.BlockSpec((d, tn), lambda i, j: (0, blk + j)),
            pl.BlockSpec((tm, tc), lambda i, j: (i, j)),
            pl.BlockSpec((CONF_PAD, tc), lambda i, j: (jnp.maximum(i * halo_blocks - 1, 0), j)),
            pl.BlockSpec((None, CONF_KERNEL, tc), lambda i, j: (layer, 0, j)),
            pl.BlockSpec((None, 1, tc), lambda i, j: (layer, 0, j)),
            pl.BlockSpec((SUBLANES, LANES), lambda i, j: (0, 0)),
        ],
        args=[h, w, u, u, conv_w, conv_b, _zero_tile()],
        out_specs=[pl.BlockSpec((tm, tn), lambda i, j: (i, j)),
                   pl.BlockSpec((tm, tc), lambda i, j: (i, j))],
        out_shapes=[jax.ShapeDtypeStruct((s, n_out), BF16), jax.ShapeDtypeStruct((s, dc), BF16)],
        scratch=[pltpu.VMEM((tc // LANES, CONF_PAD + tm, LANES), F32)],
        casts=casts, name="proj_sigmoid_conv31")
    return gates, cconv, cast_out


def _ln_swish_piece(cc_ref, lng_ref, lnb_ref, ua_ref, q0, rows):
    cc = cc_ref[q0:q0 + rows, :].astype(F32)
    mu = jnp.mean(cc, axis=-1, keepdims=True)
    dev = cc - mu
    var = jnp.mean(dev * dev, axis=-1, keepdims=True)
    y = dev * lax.rsqrt(var + LN_EPS) * lng_ref[...] + lnb_ref[...]
    y = y * _sigmoid(y)
    ua_ref[q0:q0 + rows, :] = y.astype(ua_ref.dtype)
    return y


def _proj_mul_conv_ln_kernel(h_ref, w1_ref, w2_ref, cw_ref, cc_ref, lng_ref, lnb_ref, z_ref, o_ref, ua_ref, e_ref, halo_ref):
    tm, tn = o_ref.shape
    ncb = tn // LANES
    pad = SHORT_PAD
    n_chunks = tm // ROW_CHUNK
    ln_rows = ua_ref.shape[0] // n_chunks
    _load_halo(e_ref, halo_ref, pad)
    for k in range(n_chunks):
        r0 = k * ROW_CHUNK
        hh = h_ref[r0:r0 + ROW_CHUNK, :]
        t1 = _dot(hh, w1_ref[...])
        y = _ln_swish_piece(cc_ref, lng_ref, lnb_ref, ua_ref, k * ln_rows, ln_rows)
        t2 = _dot(_tie(hh, _or_fold(y), z_ref), w2_ref[...])
        _put_rows(e_ref, t1 * t2, 0, pad + r0)
        for c in range(ncb):
            v = _conv_slab(e_ref, cw_ref, c, c, pad + r0, ROW_CHUNK, SCONV_KERNEL)
            o_ref[r0:r0 + ROW_CHUNK, c * LANES:(c + 1) * LANES] = v.astype(o_ref.dtype)
    _save_halo(e_ref, halo_ref, pad, tm)


def _proj_mul_conv_ln(h, w, blk1, blk2, n_out, conv_w, cconv, ln_g, ln_b, layer, tm=1024, tn=512):
    s, d = h.shape
    dc = cconv.shape[1]
    nj = n_out // tn
    ncb = tn // LANES
    tr = tm // nj
    (v, ua), _ = _host_call(
        _proj_mul_conv_ln_kernel, grid=(s // tm, nj),
        in_specs=[
            pl.BlockSpec((tm, d), lambda i, j: (i, 0)),
            pl.BlockSpec((d, tn), lambda i, j: (0, blk1 + j)),
            pl.BlockSpec((d, tn), lambda i, j: (0, blk2 + j)),
            pl.BlockSpec((None, SCONV_KERNEL, tn), lambda i, j: (layer, 0, j)),
            pl.BlockSpec((tr, dc), lambda i, j: (i * nj + j, 0)),
            pl.BlockSpec((None, 1, dc), lambda i, j: (layer, 0, 0)),
            pl.BlockSpec((None, 1, dc), lambda i, j: (layer, 0, 0)),
            pl.BlockSpec((SUBLANES, LANES), lambda i, j: (0, 0)),
        ],
        args=[h, w, w, conv_w, cconv, ln_g, ln_b, _zero_tile()],
        out_specs=[pl.BlockSpec((tm, tn), lambda i, j: (i, j)),
                   pl.BlockSpec((tr, dc), lambda i, j: (i * nj + j, 0))],
        out_shapes=[jax.ShapeDtypeStruct((s, n_out), BF16), jax.ShapeDtypeStruct((s, dc), BF16)],
        scratch=[pltpu.VMEM((ncb, SHORT_PAD + tm, LANES), F32),
                 pltpu.VMEM((nj, ncb, SHORT_PAD, LANES), F32)],
        name="proj_mul_conv3_ln")
    return v, ua


def _proj_mul_kernel(h_ref, w_ref, v_ref, o_ref):
    tm = o_ref.shape[0]
    for r0 in range(0, tm, ROW_CHUNK):
        t = _dot(h_ref[r0:r0 + ROW_CHUNK, :], w_ref[...])
        o_ref[r0:r0 + ROW_CHUNK, :] = (t * v_ref[r0:r0 + ROW_CHUNK, :].astype(F32)).astype(o_ref.dtype)


def _proj_mul(h, w, blk, n_out, v, tm=1024, tn=1024):
    s, d = h.shape
    (out,), _ = _host_call(
        _proj_mul_kernel, grid=(s // tm, n_out // tn),
        in_specs=[
            pl.BlockSpec((tm, d), lambda i, j: (i, 0)),
            pl.BlockSpec((d, tn), lambda i, j: (0, blk + j)),
            pl.BlockSpec((tm, tn), lambda i, j: (i, j)),
        ],
        args=[h, w, v],
        out_specs=[pl.BlockSpec((tm, tn), lambda i, j: (i, j))],
        out_shapes=[jax.ShapeDtypeStruct((s, n_out), BF16)],
        name="proj_mul")
    return out


def _mixer_kernel(ua_ref, vb_ref, ga_ref, gb_ref, wa_ref, wb_ref, o_ref):
    tm = o_ref.shape[0]
    for r0 in range(0, tm, ROW_CHUNK):
        ya = _dot(ua_ref[r0:r0 + ROW_CHUNK, :], wa_ref[...])
        yb = _dot(vb_ref[r0:r0 + ROW_CHUNK, :], wb_ref[...])
        m = ga_ref[r0:r0 + ROW_CHUNK, :].astype(F32) * ya + gb_ref[r0:r0 + ROW_CHUNK, :].astype(F32) * yb
        o_ref[r0:r0 + ROW_CHUNK, :] = m.astype(o_ref.dtype)


def _mixer(ua, vb, gates, w_a, w_b, casts, tm=1024, tn=512):
    s, dc = ua.shape
    d = w_a.shape[1]
    nj = d // tn
    (m,), cast_out = _host_call(
        _mixer_kernel, grid=(s // tm, nj),
        in_specs=[
            pl.BlockSpec((tm, dc), lambda i, j: (i, 0)),
            pl.BlockSpec((tm, dc), lambda i, j: (i, 0)),
            pl.BlockSpec((tm, tn), lambda i, j: (i, j)),
            pl.BlockSpec((tm, tn), lambda i, j: (i, nj + j)),
            pl.BlockSpec((dc, tn), lambda i, j: (0, j)),
            pl.BlockSpec((dc, tn), lambda i, j: (0, j)),
        ],
        args=[ua, vb, gates, gates, w_a, w_b],
        out_specs=[pl.BlockSpec((tm, tn), lambda i, j: (i, j))],
        out_shapes=[jax.ShapeDtypeStruct((s, d), BF16)],
        casts=casts, name="mixer")
    return m, cast_out


def _resid_kernel(a_ref, w_ref, x_ref, gt_ref, o_ref):
    tm = o_ref.shape[0]
    for r0 in range(0, tm, ROW_CHUNK):
        t = _dot(a_ref[r0:r0 + ROW_CHUNK, :], w_ref[...])
        o_ref[r0:r0 + ROW_CHUNK, :] = x_ref[r0:r0 + ROW_CHUNK, :] + gt_ref[...] * t


def _resid_proj(a, w, x, gt, tm, tn):
    s, k = a.shape
    d = x.shape[1]
    (out,), _ = _host_call(
        _resid_kernel, grid=(s // tm, d // tn),
        in_specs=[
            pl.BlockSpec((tm, k), lambda i, j: (i, 0)),
            pl.BlockSpec((k, tn), lambda i, j: (0, j)),
            pl.BlockSpec((tm, tn), lambda i, j: (i, j)),
            pl.BlockSpec((1, tn), lambda i, j: (0, j)),
        ],
        args=[a, w, x, gt],
        out_specs=[pl.BlockSpec((tm, tn), lambda i, j: (i, j))],
        out_shapes=[jax.ShapeDtypeStruct((s, d), F32)],
        name="resid_proj")
    return out


def _ffn_up_kernel(h_ref, wg_ref, wv_ref, cwg_ref, cwv_ref, o_ref, e_ref, halo_ref):
    tm, tn = o_ref.shape
    ncb = tn // LANES
    pad = SHORT_PAD
    _load_halo(e_ref, halo_ref, pad)
    for r0 in range(0, tm, ROW_CHUNK):
        hh = h_ref[r0:r0 + ROW_CHUNK, :]
        _put_rows(e_ref, _dot(hh, wg_ref[...]), 0, pad + r0)
        _put_rows(e_ref, _dot(hh, wv_ref[...]), ncb, pad + r0)
        for c in range(ncb):
            fg = _conv_slab(e_ref, cwg_ref, c, c, pad + r0, ROW_CHUNK, FFN_KERNEL)
            fv = _conv_slab(e_ref, cwv_ref, ncb + c, c, pad + r0, ROW_CHUNK, FFN_KERNEL)
            o_ref[r0:r0 + ROW_CHUNK, c * LANES:(c + 1) * LANES] = (fg * _sigmoid(fg) * fv).astype(o_ref.dtype)
    _save_halo(e_ref, halo_ref, pad, tm)


def _ffn_up(h, wg, wv, conv_w, layer, casts, tm=2048, tn=256):
    s, d = h.shape
    d_ff = wg.shape[1]
    nj = d_ff // tn
    ncb = tn // LANES
    (g,), cast_out = _host_call(
        _ffn_up_kernel, grid=(s // tm, nj),
        in_specs=[
            pl.BlockSpec((tm, d), lambda i, j: (i, 0), pipeline_mode=pl.Buffered(1)),
            pl.BlockSpec((d, tn), lambda i, j: (0, j)),
            pl.BlockSpec((d, tn), lambda i, j: (0, j)),
            pl.BlockSpec((None, FFN_KERNEL, tn), lambda i, j: (layer, 0, j)),
            pl.BlockSpec((None, FFN_KERNEL, tn), lambda i, j: (layer, 0, nj + j)),
        ],
        args=[h, wg, wv, conv_w, conv_w],
        out_specs=[pl.BlockSpec((tm, tn), lambda i, j: (i, j))],
        out_shapes=[jax.ShapeDtypeStruct((s, d_ff), BF16)],
        scratch=[pltpu.VMEM((2 * ncb, SHORT_PAD + tm, LANES), F32),
                 pltpu.VMEM((nj, 2 * ncb, SHORT_PAD, LANES), F32)],
        casts=casts, name="ffn_up")
    return g, cast_out


def kernel(x, c, w_ada, b_ada, g_mix, w_in, conf_w, conf_b, ln_g, ln_b, w_a_out, sconv_w, w_b_out, w_o, g_ffn, w_up, ffn_conv_w, w_down, g_final):
    batch, seq, d = x.shape
    assert batch == 1
    n_layers = w_ada.shape[0]
    d_conf = conf_w.shape[2]
    d_ff = w_down.shape[1]
    n_in = w_in.shape[2]
    nb = d_conf // 512

    mod = _ada_mod(c, w_ada, b_ada).reshape(n_layers, 6, 1, d)
    conf_b3 = conf_b.reshape(n_layers, 1, d_conf)
    ln_g3 = ln_g.reshape(n_layers, 1, d_conf)
    ln_b3 = ln_b.reshape(n_layers, 1, d_conf)

    xs = x.reshape(seq, d)
    w_in_b = w_in[0].astype(BF16)
    for l in range(n_layers):
        sh1, sc1, gt1, sh2, sc2, gt2 = (mod[l, k] for k in range(6))

        h = _norm_mod(xs, g_mix[l].reshape(1, d), sc1, sh1)
        u, (w_a_b, w_b_b, w_o_b) = _proj_glu(
            h, w_in_b, 0, nb, d_conf,
            casts=[CastJob(w_a_out, l, 0, d), CastJob(w_b_out, l, 0, d), CastJob(w_o, l, 0, d)])
        gates, cconv, (wg_b,) = _proj_sigmoid_conv(
            h, w_in_b, 5 * d_conf // 1024, 2 * d, u, conf_w, conf_b3, l,
            casts=[CastJob(w_up, l, 0, d_ff)])
        v, ua = _proj_mul_conv_ln(h, w_in_b, 3 * nb, 4 * nb, d_conf, sconv_w, cconv, ln_g3, ln_b3, l)
        vb = _proj_mul(h, w_in_b, 2 * d_conf // 1024, d_conf, v)
        m, (wv_b,) = _mixer(ua, vb, gates, w_a_b, w_b_b, casts=[CastJob(w_up, l, 1, d_ff)])
        xs = _resid_proj(m, w_o_b, xs, gt1, tm=1024, tn=1024)

        h2 = _norm_mod(xs, g_ffn[l].reshape(1, d), sc2, sh2)
        casts = [CastJob(w_down, l, 0, d)]
        if l + 1 < n_layers:
            casts.append(CastJob(w_in, l + 1, 0, n_in))
        g, cast_out = _ffn_up(h2, wg_b, wv_b, ffn_conv_w, l, casts)
        w_down_b = cast_out[0]
        if l + 1 < n_layers:
            w_in_b = cast_out[1]
        xs = _resid_proj(g, w_down_b, xs, gt2, tm=512, tn=512)

    return _final_norm(xs, g_final.reshape(1, d)).reshape(batch, seq, d)
```

```python
import collections
import functools

import jax
import jax.numpy as jnp
from jax import lax
from jax.experimental import pallas as pl
from jax.experimental.pallas import tpu as pltpu

F32 = jnp.float32
BF16 = jnp.bfloat16

RMS_EPS = 1e-6
LN_EPS = 1e-5
CONF_KERNEL = 31
SCONV_KERNEL = 3
FFN_KERNEL = 3

LANES = 128
SUBLANES = 8
VMEM_LIMIT = 56 * 1024 * 1024

ROW_CHUNK = 256
CONF_PAD = 32
SHORT_PAD = 8
NORM_ROWS = 32
CONF_ROWS = 64


def _params(sem):
    return pltpu.CompilerParams(dimension_semantics=sem, vmem_limit_bytes=VMEM_LIMIT)


def _sigmoid(x):
    return jax.nn.sigmoid(x)


def _dot(a, b):
    return jnp.dot(a, b, preferred_element_type=F32)


def _or_fold(y):
    bits = pltpu.bitcast(y, jnp.uint32)
    acc = None
    for r in range(0, bits.shape[0], SUBLANES):
        for c in range(0, bits.shape[1], LANES):
            t = bits[r:r + SUBLANES, c:c + LANES]
            acc = t if acc is None else acc | t
    return acc


def _zero_tile():
    return jnp.zeros((SUBLANES, LANES), jnp.uint32)


def _tie(lhs, fold, zero_ref):
    bits = pltpu.bitcast(lhs, jnp.uint32)
    zero = fold & zero_ref[...]
    bits = bits | jnp.tile(zero, (bits.shape[0] // SUBLANES, bits.shape[1] // LANES))
    return pltpu.bitcast(bits, lhs.dtype)


CastJob = collections.namedtuple("CastJob", "w layer col_block cols")
BF16_ROWS = 16


def _cast_rows(n_rows, n_steps):
    rows = BF16_ROWS
    while n_rows % rows or n_rows // rows > n_steps:
        rows += BF16_ROWS
    return rows


def _host_call(body, *, grid, in_specs, args, out_specs, out_shapes, scratch=(), casts=(), name):
    gi, gj = grid
    n_in, n_out, n_cast = len(args), len(out_specs), len(casts)
    in_specs, out_specs, out_shapes, args = list(in_specs), list(out_specs), list(out_shapes), list(args)
    for job in casts:
        n_rows = job.w.shape[1]
        rows = _cast_rows(n_rows, gi * gj)
        n_blocks = n_rows // rows

        def block(i, j, n_blocks=n_blocks):
            return jnp.minimum(i * gj + j, n_blocks - 1)

        in_specs.append(pl.BlockSpec((None, rows, job.cols),
                                     lambda i, j, job=job, block=block: (job.layer, block(i, j), job.col_block)))
        out_specs.append(pl.BlockSpec((rows, job.cols), lambda i, j, block=block: (block(i, j), 0)))
        out_shapes.append(jax.ShapeDtypeStruct((n_rows, job.cols), BF16))
        args.append(job.w)

    def wrapped(*refs):
        main_in = refs[:n_in]
        cast_in = refs[n_in:n_in + n_cast]
        main_out = refs[n_in + n_cast:n_in + n_cast + n_out]
        cast_out = refs[n_in + n_cast + n_out:n_in + 2 * n_cast + n_out]
        rest = refs[n_in + 2 * n_cast + n_out:]
        for src, dst in zip(cast_in, cast_out):
            dst[...] = src[...].astype(BF16)
        body(*main_in, *main_out, *rest)

    outs = pl.pallas_call(
        wrapped, grid=grid, in_specs=in_specs, out_specs=out_specs, out_shape=out_shapes,
        scratch_shapes=list(scratch), compiler_params=_params(("arbitrary", "arbitrary")), name=name,
    )(*args)
    return outs[:n_out], outs[n_out:]


def _ada_kernel(c_ref, w_ref, b_ref, o_ref):
    c = c_ref[...]
    ca = (c * _sigmoid(c)).astype(BF16)
    o_ref[...] = _dot(ca, w_ref[...].astype(BF16)) + b_ref[...]


def _ada_mod(c, w_ada, b_ada):
    n_layers, d, n = w_ada.shape
    tn = 1024
    c8 = jnp.broadcast_to(c.reshape(1, d), (SUBLANES, d))
    out = pl.pallas_call(
        _ada_kernel,
        grid=(n_layers, n // tn),
        in_specs=[
            pl.BlockSpec((SUBLANES, d), lambda l, j: (0, 0)),
            pl.BlockSpec((None, d, tn), lambda l, j: (l, 0, j)),
            pl.BlockSpec((None, 1, tn), lambda l, j: (l, 0, j)),
        ],
        out_specs=pl.BlockSpec((None, SUBLANES, tn), lambda l, j: (l, 0, j)),
        out_shape=jax.ShapeDtypeStruct((n_layers, SUBLANES, n), F32),
        compiler_params=_params(("arbitrary", "arbitrary")),
        name="ada_mod",
    )(c8, w_ada, b_ada.reshape(n_layers, 1, n))
    return out[:, 0, :]


def _norm_mod_kernel(x_ref, g_ref, sc_ref, sh_ref, o_ref):
    gain = g_ref[...] * (1.0 + sc_ref[...])
    for r0 in range(0, o_ref.shape[0], NORM_ROWS):
        x = x_ref[r0:r0 + NORM_ROWS, :]
        ms = jnp.mean(x * x, axis=-1, keepdims=True)
        o_ref[r0:r0 + NORM_ROWS, :] = (x * lax.rsqrt(ms + RMS_EPS) * gain + sh_ref[...]).astype(o_ref.dtype)


def _norm_kernel(x_ref, g_ref, o_ref):
    for r0 in range(0, o_ref.shape[0], NORM_ROWS):
        x = x_ref[r0:r0 + NORM_ROWS, :]
        ms = jnp.mean(x * x, axis=-1, keepdims=True)
        o_ref[r0:r0 + NORM_ROWS, :] = (x * lax.rsqrt(ms + RMS_EPS) * g_ref[...]).astype(o_ref.dtype)


def _norm_mod(x, g, sc, sh):
    s, d = x.shape
    tr = 1024
    row = pl.BlockSpec((tr, d), lambda i: (i, 0))
    vec = pl.BlockSpec((1, d), lambda i: (0, 0))
    return pl.pallas_call(
        _norm_mod_kernel, grid=(s // tr,),
        in_specs=[row, vec, vec, vec], out_specs=row,
        out_shape=jax.ShapeDtypeStruct((s, d), BF16),
        compiler_params=_params(("arbitrary",)), name="norm_mod",
    )(x, g, sc, sh)


def _final_norm(x, g):
    s, d = x.shape
    tr = 512
    row = pl.BlockSpec((tr, d), lambda i: (i, 0))
    vec = pl.BlockSpec((1, d), lambda i: (0, 0))
    return pl.pallas_call(
        _norm_kernel, grid=(s // tr,),
        in_specs=[row, vec], out_specs=row,
        out_shape=jax.ShapeDtypeStruct((s, d), F32),
        compiler_params=_params(("arbitrary",)), name="final_norm",
    )(x, g)


def _load_halo(e_ref, halo_ref, pad):
    i = pl.program_id(0)
    j = pl.program_id(1)

    @pl.when(i == 0)
    def _():
        e_ref[:, 0:pad, :] = jnp.zeros((e_ref.shape[0], pad, LANES), F32)

    @pl.when(i > 0)
    def _():
        e_ref[:, 0:pad, :] = halo_ref[j]


def _save_halo(e_ref, halo_ref, pad, tm):
    halo_ref[pl.program_id(1)] = e_ref[:, tm:tm + pad, :]


def _put_rows(e_ref, u, slab0, row0):
    for c in range(u.shape[1] // LANES):
        e_ref[slab0 + c, row0:row0 + u.shape[0], :] = u[:, c * LANES:(c + 1) * LANES]


def _conv_slab(e_ref, cw_ref, slab, col, row0, rows, k_taps):
    acc = None
    for kk in range(k_taps):
        shift = k_taps - 1 - kk
        term = e_ref[slab, row0 - shift:row0 - shift + rows, :] * cw_ref[kk:kk + 1, col * LANES:(col + 1) * LANES]
        acc = term if acc is None else acc + term
    return acc


def _conv_slab_blocked(e_ref, cw_ref, slab, col, row0, rows, k_taps):
    acc = None
    span = ((k_taps - 1) // SUBLANES) * SUBLANES
    for b in range(SUBLANES):
        win = e_ref[slab, row0 - b - span:row0 - b + rows, :]
        for a in range(span // SUBLANES + 1):
            shift = a * SUBLANES + b
            if shift > k_taps - 1:
                continue
            off = span - a * SUBLANES
            kk = k_taps - 1 - shift
            term = win[off:off + rows, :] * cw_ref[kk:kk + 1, col * LANES:(col + 1) * LANES]
            acc = term if acc is None else acc + term
    return acc


def _proj_glu_kernel(h_ref, w1_ref, w2_ref, o_ref):
    tm = o_ref.shape[0]
    for r0 in range(0, tm, ROW_CHUNK):
        hh = h_ref[r0:r0 + ROW_CHUNK, :]
        t1 = _dot(hh, w1_ref[...])
        t2 = _dot(hh, w2_ref[...])
        o_ref[r0:r0 + ROW_CHUNK, :] = (t1 * _sigmoid(t2)).astype(o_ref.dtype)


def _proj_glu(h, w, blk1, blk2, n_out, casts, tm=1024, tn=512):
    s, d = h.shape
    (u,), cast_out = _host_call(
        _proj_glu_kernel, grid=(s // tm, n_out // tn),
        in_specs=[
            pl.BlockSpec((tm, d), lambda i, j: (i, 0)),
            pl.BlockSpec((d, tn), lambda i, j: (0, blk1 + j)),
            pl.BlockSpec((d, tn), lambda i, j: (0, blk2 + j)),
        ],
        args=[h, w, w],
        out_specs=[pl.BlockSpec((tm, tn), lambda i, j: (i, j))],
        out_shapes=[jax.ShapeDtypeStruct((s, n_out), BF16)],
        casts=casts, name="proj_glu")
    return u, cast_out


def _proj_sigmoid_conv_kernel(h_ref, w_ref, um_ref, uh_ref, cw_ref, cb_ref, z_ref, o_ref, c_ref, e_ref):
    tm, tn = o_ref.shape
    half = tn // 2
    ncb = c_ref.shape[1] // LANES
    pad = CONF_PAD
    first = pl.program_id(0) == 0
    _put_rows(e_ref, jnp.where(first, 0.0, uh_ref[...].astype(F32)), 0, 0)
    for r0 in range(0, tm, ROW_CHUNK):
        _put_rows(e_ref, um_ref[r0:r0 + ROW_CHUNK, :].astype(F32), 0, pad + r0)
    for r0 in range(0, tm, ROW_CHUNK):
        hh = h_ref[r0:r0 + ROW_CHUNK, :]
        t = _dot(hh, w_ref[:, 0:half])
        o_ref[r0:r0 + ROW_CHUNK, 0:half] = _sigmoid(t).astype(o_ref.dtype)
        fold = None
        for c in range(ncb):
            for q0 in range(r0, r0 + ROW_CHUNK, CONF_ROWS):
                y = _conv_slab_blocked(e_ref, cw_ref, c, c, pad + q0, CONF_ROWS, CONF_KERNEL)
                y = y + cb_ref[:, c * LANES:(c + 1) * LANES]
                c_ref[q0:q0 + CONF_ROWS, c * LANES:(c + 1) * LANES] = y.astype(c_ref.dtype)
                f = _or_fold(y)
                fold = f if fold is None else fold | f
        t = _dot(_tie(hh, fold, z_ref), w_ref[:, half:tn])
        o_ref[r0:r0 + ROW_CHUNK, half:tn] = _sigmoid(t).astype(o_ref.dtype)


def _proj_sigmoid_conv(h, w, blk, n_out, u, conv_w, conv_b, layer, casts, tm=1024, tn=1024):
    s, d = h.shape
    dc = u.shape[1]
    nj = n_out // tn
    tc = dc // nj
    halo_blocks = tm // CONF_PAD
    (gates, cconv), cast_out = _host_call(
        _proj_sigmoid_conv_kernel, grid=(s // tm, nj),
        in_specs=[
            pl.BlockSpec((tm, d), lambda i, j: (i, 0)),
            pl.BlockSpec((d, tn), lambda i, j: (0, blk + j)),
            pl.BlockSpec((tm, tc), lambda i, j: (i, j)),
            pl.BlockSpec((CONF_PAD, tc), lambda i, j: (jnp.maximum(i * halo_blocks - 1, 0), j)),
            pl.BlockSpec((None, CONF_KERNEL, tc), lambda i, j: (layer, 0, j)),
            pl.BlockSpec((None, 1, tc), lambda i, j: (layer, 0, j)),
            pl.BlockSpec((SUBLANES, LANES), lambda i, j: (0, 0)),
        ],
        args=[h, w, u, u, conv_w, conv_b, _zero_tile()],
        out_specs=[pl.BlockSpec((tm, tn), lambda i, j: (i, j)),
                   pl.BlockSpec((tm, tc), lambda i, j: (i, j))],
        out_shapes=[jax.ShapeDtypeStruct((s, n_out), BF16), jax.ShapeDtypeStruct((s, dc), BF16)],
        scratch=[pltpu.VMEM((tc // LANES, CONF_PAD + tm, LANES), F32)],
        casts=casts, name="proj_sigmoid_conv31")
    return gates, cconv, cast_out


def _ln_swish_piece(cc_ref, lng_ref, lnb_ref, ua_ref, q0, rows):
    cc = cc_ref[q0:q0 + rows, :].astype(F32)
    mu = jnp.mean(cc, axis=-1, keepdims=True)
    dev = cc - mu
    var = jnp.mean(dev * dev, axis=-1, keepdims=True)
    y = dev * lax.rsqrt(var + LN_EPS) * lng_ref[...] + lnb_ref[...]
    y = y * _sigmoid(y)
    ua_ref[q0:q0 + rows, :] = y.astype(ua_ref.dtype)
    return y


def _proj_mul_conv_ln_kernel(h_ref, w1_ref, w2_ref, cw_ref, cc_ref, lng_ref, lnb_ref, z_ref, o_ref, ua_ref, e_ref, halo_ref):
    tm, tn = o_ref.shape
    ncb = tn // LANES
    pad = SHORT_PAD
    n_chunks = tm // ROW_CHUNK
    ln_rows = ua_ref.shape[0] // n_chunks
    _load_halo(e_ref, halo_ref, pad)
    for k in range(n_chunks):
        r0 = k * ROW_CHUNK
        hh = h_ref[r0:r0 + ROW_CHUNK, :]
        t1 = _dot(hh, w1_ref[...])
        y = _ln_swish_piece(cc_ref, lng_ref, lnb_ref, ua_ref, k * ln_rows, ln_rows)
        t2 = _dot(_tie(hh, _or_fold(y), z_ref), w2_ref[...])
        _put_rows(e_ref, t1 * t2, 0, pad + r0)
        for c in range(ncb):
            v = _conv_slab(e_ref, cw_ref, c, c, pad + r0, ROW_CHUNK, SCONV_KERNEL)
            o_ref[r0:r0 + ROW_CHUNK, c * LANES:(c + 1) * LANES] = v.astype(o_ref.dtype)
    _save_halo(e_ref, halo_ref, pad, tm)


def _proj_mul_conv_ln(h, w, blk1, blk2, n_out, conv_w, cconv, ln_g, ln_b, layer, tm=1024, tn=512):
    s, d = h.shape
    dc = cconv.shape[1]
    nj = n_out // tn
    ncb = tn // LANES
    tr = tm // nj
    (v, ua), _ = _host_call(
        _proj_mul_conv_ln_kernel, grid=(s // tm, nj),
        in_specs=[
            pl.BlockSpec((tm, d), lambda i, j: (i, 0)),
            pl.BlockSpec((d, tn), lambda i, j: (0, blk1 + j)),
            pl.BlockSpec((d, tn), lambda i, j: (0, blk2 + j)),
            pl.BlockSpec((None, SCONV_KERNEL, tn), lambda i, j: (layer, 0, j)),
            pl.BlockSpec((tr, dc), lambda i, j: (i * nj + j, 0)),
            pl.BlockSpec((None, 1, dc), lambda i, j: (layer, 0, 0)),
            pl.BlockSpec((None, 1, dc), lambda i, j: (layer, 0, 0)),
            pl.BlockSpec((SUBLANES, LANES), lambda i, j: (0, 0)),
        ],
        args=[h, w, w, conv_w, cconv, ln_g, ln_b, _zero_tile()],
        out_specs=[pl.BlockSpec((tm, tn), lambda i, j: (i, j)),
                   pl.BlockSpec((tr, dc), lambda i, j: (i * nj + j, 0))],
        out_shapes=[jax.ShapeDtypeStruct((s, n_out), BF16), jax.ShapeDtypeStruct((s, dc), BF16)],
        scratch=[pltpu.VMEM((ncb, SHORT_PAD + tm, LANES), F32),
                 pltpu.VMEM((nj, ncb, SHORT_PAD, LANES), F32)],
        name="proj_mul_conv3_ln")
    return v, ua


def _proj_mul_kernel(h_ref, w_ref, v_ref, o_ref):
    tm = o_ref.shape[0]
    for r0 in range(0, tm, ROW_CHUNK):
        t = _dot(h_ref[r0:r0 + ROW_CHUNK, :], w_ref[...])
        o_ref[r0:r0 + ROW_CHUNK, :] = (t * v_ref[r0:r0 + ROW_CHUNK, :].astype(F32)).astype(o_ref.dtype)


def _proj_mul(h, w, blk, n_out, v, tm=1024, tn=1024):
    s, d = h.shape
    (out,), _ = _host_call(
        _proj_mul_kernel, grid=(s // tm, n_out // tn),
        in_specs=[
            pl.BlockSpec((tm, d), lambda i, j: (i, 0)),
            pl.BlockSpec((d, tn), lambda i, j: (0, blk + j)),
            pl.BlockSpec((tm, tn), lambda i, j: (i, j)),
        ],
        args=[h, w, v],
        out_specs=[pl.BlockSpec((tm, tn), lambda i, j: (i, j))],
        out_shapes=[jax.ShapeDtypeStruct((s, n_out), BF16)],
        name="proj_mul")
    return out


def _mixer_kernel(ua_ref, vb_ref, ga_ref, gb_ref, wa_ref, wb_ref, o_ref):
    tm = o_ref.shape[0]
    for r0 in range(0, tm, ROW_CHUNK):
        ya = _dot(ua_ref[r0:r0 + ROW_CHUNK, :], wa_ref[...])
        yb = _dot(vb_ref[r0:r0 + ROW_CHUNK, :], wb_ref[...])
        m = ga_ref[r0:r0 + ROW_CHUNK, :].astype(F32) * ya + gb_ref[r0:r0 + ROW_CHUNK, :].astype(F32) * yb
        o_ref[r0:r0 + ROW_CHUNK, :] = m.astype(o_ref.dtype)


def _mixer(ua, vb, gates, w_a, w_b, casts, tm=1024, tn=512):
    s, dc = ua.shape
    d = w_a.shape[1]
    nj = d // tn
    (m,), cast_out = _host_call(
        _mixer_kernel, grid=(s // tm, nj),
        in_specs=[
            pl.BlockSpec((tm, dc), lambda i, j: (i, 0)),
            pl.BlockSpec((tm, dc), lambda i, j: (i, 0)),
            pl.BlockSpec((tm, tn), lambda i, j: (i, j)),
            pl.BlockSpec((tm, tn), lambda i, j: (i, nj + j)),
            pl.BlockSpec((dc, tn), lambda i, j: (0, j)),
            pl.BlockSpec((dc, tn), lambda i, j: (0, j)),
        ],
        args=[ua, vb, gates, gates, w_a, w_b],
        out_specs=[pl.BlockSpec((tm, tn), lambda i, j: (i, j))],
        out_shapes=[jax.ShapeDtypeStruct((s, d), BF16)],
        casts=casts, name="mixer")
    return m, cast_out


def _resid_kernel(a_ref, w_ref, x_ref, gt_ref, o_ref):
    tm = o_ref.shape[0]
    for r0 in range(0, tm, ROW_CHUNK):
        t = _dot(a_ref[r0:r0 + ROW_CHUNK, :], w_ref[...])
        o_ref[r0:r0 + ROW_CHUNK, :] = x_ref[r0:r0 + ROW_CHUNK, :] + gt_ref[...] * t


def _resid_proj(a, w, x, gt, tm, tn):
    s, k = a.shape
    d = x.shape[1]
    (out,), _ = _host_call(
        _resid_kernel, grid=(s // tm, d // tn),
        in_specs=[
            pl.BlockSpec((tm, k), lambda i, j: (i, 0)),
            pl.BlockSpec((k, tn), lambda i, j: (0, j)),
            pl.BlockSpec((tm, tn), lambda i, j: (i, j)),
            pl.BlockSpec((1, tn), lambda i, j: (0, j)),
        ],
        args=[a, w, x, gt],
        out_specs=[pl.BlockSpec((tm, tn), lambda i, j: (i, j))],
        out_shapes=[jax.ShapeDtypeStruct((s, d), F32)],
        name="resid_proj")
    return out


def _ffn_up_kernel(h_ref, wg_ref, wv_ref, cwg_ref, cwv_ref, o_ref, e_ref, halo_ref):
    tm, tn = o_ref.shape
    ncb = tn // LANES
    pad = SHORT_PAD
    _load_halo(e_ref, halo_ref, pad)
    for r0 in range(0, tm, ROW_CHUNK):
        hh = h_ref[r0:r0 + ROW_CHUNK, :]
        _put_rows(e_ref, _dot(hh, wg_ref[...]), 0, pad + r0)
        _put_rows(e_ref, _dot(hh, wv_ref[...]), ncb, pad + r0)
        for c in range(ncb):
            fg = _conv_slab(e_ref, cwg_ref, c, c, pad + r0, ROW_CHUNK, FFN_KERNEL)
            fv = _conv_slab(e_ref, cwv_ref, ncb + c, c, pad + r0, ROW_CHUNK, FFN_KERNEL)
            o_ref[r0:r0 + ROW_CHUNK, c * LANES:(c + 1) * LANES] = (fg * _sigmoid(fg) * fv).astype(o_ref.dtype)
    _save_halo(e_ref, halo_ref, pad, tm)


def _ffn_up(h, wg, wv, conv_w, layer, casts, tm=2048, tn=256):
    s, d = h.shape
    d_ff = wg.shape[1]
    nj = d_ff // tn
    ncb = tn // LANES
    (g,), cast_out = _host_call(
        _ffn_up_kernel, grid=(s // tm, nj),
        in_specs=[
            pl.BlockSpec((tm, d), lambda i, j: (i, 0), pipeline_mode=pl.Buffered(1)),
            pl.BlockSpec((d, tn), lambda i, j: (0, j)),
            pl.BlockSpec((d, tn), lambda i, j: (0, j)),
            pl.BlockSpec((None, FFN_KERNEL, tn), lambda i, j: (layer, 0, j)),
            pl.BlockSpec((None, FFN_KERNEL, tn), lambda i, j: (layer, 0, nj + j)),
        ],
        args=[h, wg, wv, conv_w, conv_w],
        out_specs=[pl.BlockSpec((tm, tn), lambda i, j: (i, j))],
        out_shapes=[jax.ShapeDtypeStruct((s, d_ff), BF16)],
        scratch=[pltpu.VMEM((2 * ncb, SHORT_PAD + tm, LANES), F32),
                 pltpu.VMEM((nj, 2 * ncb, SHORT_PAD, LANES), F32)],
        casts=casts, name="ffn_up")
    return g, cast_out


def kernel(x, c, w_ada, b_ada, g_mix, w_in, conf_w, conf_b, ln_g, ln_b, w_a_out, sconv_w, w_b_out, w_o, g_ffn, w_up, ffn_conv_w, w_down, g_final):
    batch, seq, d = x.shape
    assert batch == 1
    n_layers = w_ada.shape[0]
    d_conf = conf_w.shape[2]
    d_ff = w_down.shape[1]
    n_in = w_in.shape[2]
    nb = d_conf // 512

    mod = _ada_mod(c, w_ada, b_ada).reshape(n_layers, 6, 1, d)
    conf_b3 = conf_b.reshape(n_layers, 1, d_conf)
    ln_g3 = ln_g.reshape(n_layers, 1, d_conf)
    ln_b3 = ln_b.reshape(n_layers, 1, d_conf)

    xs = x.reshape(seq, d)
    w_in_b = w_in[0].astype(BF16)
    for l in range(n_layers):
        sh1, sc1, gt1, sh2, sc2, gt2 = (mod[l, k] for k in range(6))

        h = _norm_mod(xs, g_mix[l].reshape(1, d), sc1, sh1)
        u, (w_a_b, w_b_b, w_o_b) = _proj_glu(
            h, w_in_b, 0, nb, d_conf,
            casts=[CastJob(w_a_out, l, 0, d), CastJob(w_b_out, l, 0, d), CastJob(w_o, l, 0, d)])
        gates, cconv, (wg_b,) = _proj_sigmoid_conv(
            h, w_in_b, 5 * d_conf // 1024, 2 * d, u, conf_w, conf_b3, l,
            casts=[CastJob(w_up, l, 0, d_ff)])
        v, ua = _proj_mul_conv_ln(h, w_in_b, 3 * nb, 4 * nb, d_conf, sconv_w, cconv, ln_g3, ln_b3, l)
        vb = _proj_mul(h, w_in_b, 2 * d_conf // 1024, d_conf, v)
        m, (wv_b,) = _mixer(ua, vb, gates, w_a_b, w_b_b, casts=[CastJob(w_up, l, 1, d_ff)])
        xs = _resid_proj(m, w_o_b, xs, gt1, tm=1024, tn=1024)

        h2 = _norm_mod(xs, g_ffn[l].reshape(1, d), sc2, sh2)
        casts = [CastJob(w_down, l, 0, d)]
        if l + 1 < n_layers:
            casts.append(CastJob(w_in, l + 1, 0, n_in))
        g, cast_out = _ffn_up(h2, wg_b, wv_b, ffn_conv_w, l, casts)
        w_down_b = cast_out[0]
        if l + 1 < n_layers:
            w_in_b = cast_out[1]
        xs = _resid_proj(g, w_down_b, xs, gt2, tm=512, tn=512)

    return _final_norm(xs, g_final.reshape(1, d)).reshape(batch, seq, d)
```
